```python
import jax, jax.numpy as jnp
from jax import lax
import numpy as np

D_MODEL = 1024
BATCH = 8
SEQ = 2048
DEPTH = 2

GRID_W = 64
CTX_LEN = 256
HEAD_DIM = 64
N_Q_HEADS = (D_MODEL // 2) // HEAD_DIM
N_KV_HEADS = 2
Q_PER_KV = N_Q_HEADS // N_KV_HEADS
Q_DIM = N_Q_HEADS * HEAD_DIM
KV_DIM = N_KV_HEADS * HEAD_DIM
WINDOW = 128
ATT_BLOCK = 128
ROPE_THETA = 10000.0
ROPE_PAIRS_PER_AXIS = HEAD_DIM // 4
CONV_CH = D_MODEL // 2
CONV_WIDTH = 31
EVEN_IN = 2 * CONV_CH + Q_DIM + 2 * KV_DIM
EVEN_MIX = CONV_CH + Q_DIM
KV_COL = 2 * CONV_CH + Q_DIM
CHUNK = 128
GMLP_CH = D_MODEL
GMLP_GROUPS = 8
GMLP_GROUP_CH = GMLP_CH // GMLP_GROUPS
N_EXPERTS = 32
TOP_K = 4
EXPERT_FF = D_MODEL
SWIGLU_LIMIT = 7.0
SWIGLU_ALPHA = 1.702
MOE_BLOCK = 128

EPS = 1e-6
NEG_INF = -1e30

kernel_name = "hybrid_conv_swa_gmlp_moe_dit"


def rms_norm(x, g):
    xf = x.astype(jnp.float32)
    y = xf * lax.rsqrt(jnp.mean(xf * xf, axis=-1, keepdims=True) + EPS)
    return (y * g).astype(x.dtype)


def layer_norm(x, g, b):
    xf = x.astype(jnp.float32)
    mu = jnp.mean(xf, axis=-1, keepdims=True)
    var = jnp.mean(jnp.square(xf - mu), axis=-1, keepdims=True)
    return ((xf - mu) * lax.rsqrt(var + EPS) * g + b).astype(x.dtype)


def ada_modulation(cvec, w, b):
    return jnp.split(jax.nn.silu(cvec) @ w + b, 6, axis=-1)


def modulate(x, g, shift, scale):
    return rms_norm(x, g) * (1.0 + scale) + shift


def axial_rope(rows):
    r = jnp.repeat(jnp.arange(rows), GRID_W).astype(jnp.float32)
    col = jnp.tile(jnp.arange(GRID_W), rows).astype(jnp.float32)
    inv_freq = ROPE_THETA ** (-jnp.arange(ROPE_PAIRS_PER_AXIS, dtype=jnp.float32) / ROPE_PAIRS_PER_AXIS)
    ang = jnp.concatenate([r[:, None] * inv_freq, col[:, None] * inv_freq], axis=-1)
    return jnp.cos(ang), jnp.sin(ang)


def apply_rope(x, cos, sin):
    xf = x.astype(jnp.float32)
    x1, x2 = jnp.split(xf, 2, axis=-1)
    c, s = cos[None, :, None, :], sin[None, :, None, :]
    return jnp.concatenate([x1 * c - x2 * s, x2 * c + x1 * s], axis=-1).astype(x.dtype)


def sink_softmax(logits, sink_logit):
    sink_b = jnp.broadcast_to(sink_logit, logits.shape[:-1] + (1,))
    p = jax.nn.softmax(jnp.concatenate([logits, sink_b], axis=-1), axis=-1)
    return p[..., :-1]


def conformer_conv(a_in, conv_w, conv_b, ln_g, ln_b):
    u, gate = jnp.split(a_in, 2, axis=-1)
    u = u * jax.nn.sigmoid(gate)
    u = lax.conv_general_dilated(
        u, conv_w[:, None, :], window_strides=(1,),
        padding=[(CONV_WIDTH // 2, CONV_WIDTH // 2)],
        dimension_numbers=('NWC', 'WIO', 'NWC'),
        feature_group_count=CONV_CH) + conv_b
    return jax.nn.silu(layer_norm(u, ln_g, ln_b))


def windowed_attention(q, k, v, k_c, v_c, sink):
    bsz, n = q.shape[0], q.shape[1]
    nb = n // ATT_BLOCK
    scale = HEAD_DIM ** -0.5
    qb = q.reshape(bsz, nb, ATT_BLOCK, N_KV_HEADS, Q_PER_KV, HEAD_DIM)

    def band(t):
        tp = jnp.pad(t, ((0, 0), (ATT_BLOCK, ATT_BLOCK), (0, 0), (0, 0)))
        tp = tp.reshape(bsz, nb + 2, ATT_BLOCK, N_KV_HEADS, HEAD_DIM)
        return jnp.concatenate([tp[:, :-2], tp[:, 1:-1], tp[:, 2:]], axis=2)

    kw, vw = band(k), band(v)
    s_win = jnp.einsum('bnqkgd,bnpkd->bnkgqp', qb, kw).astype(jnp.float32) * scale
    s_ctx = jnp.einsum('bnqkgd,bpkd->bnkgqp', qb, k_c).astype(jnp.float32) * scale
    qi = jnp.arange(ATT_BLOCK)[:, None]
    kp = jnp.arange(3 * ATT_BLOCK)[None, :]
    kpos = jnp.arange(nb)[:, None, None] * ATT_BLOCK - ATT_BLOCK + kp
    valid = (jnp.abs(kp - ATT_BLOCK - qi) <= WINDOW)[None] & (kpos >= 0) & (kpos < n)
    s_win = jnp.where(valid[None, :, None, None], s_win, NEG_INF)
    sink_l = sink.reshape(N_KV_HEADS, Q_PER_KV, 1, 1).astype(jnp.float32)
    p = sink_softmax(jnp.concatenate([s_win, s_ctx], axis=-1), sink_l).astype(v.dtype)
    p_win, p_ctx = p[..., :3 * ATT_BLOCK], p[..., 3 * ATT_BLOCK:]
    out = (jnp.einsum('bnkgqp,bnpkd->bnqkgd', p_win, vw)
           + jnp.einsum('bnkgqp,bpkd->bnqkgd', p_ctx, v_c))
    return out.reshape(bsz, n, Q_DIM)


def context_attention(q_c, k_c, v_c, sink):
    bsz, nl = q_c.shape[0], q_c.shape[1]
    qg = q_c.reshape(bsz, nl, N_KV_HEADS, Q_PER_KV, HEAD_DIM)
    s = jnp.einsum('bqkgd,bpkd->bkgqp', qg, k_c).astype(jnp.float32) * (HEAD_DIM ** -0.5)
    p = sink_softmax(s, sink.reshape(N_KV_HEADS, Q_PER_KV, 1, 1).astype(jnp.float32)).astype(v_c.dtype)
    return jnp.einsum('bkgqp,bpkd->bqkgd', p, v_c).reshape(bsz, nl, Q_DIM)


def even_mixer(hm, gm, w_in, b_in, conv_w, conv_b, ln_g, ln_b, qn_g, kn_g, sink, w_out, b_out,
               cos, sin, ctx_out):
    bsz, n, _ = hm.shape
    splits = [2 * CONV_CH, KV_COL, KV_COL + KV_DIM]
    a_in, q, k, v = jnp.split(hm @ w_in + b_in, splits, axis=-1)
    conv_h = conformer_conv(a_in, conv_w, conv_b, ln_g, ln_b)
    q = apply_rope(rms_norm(q.reshape(bsz, n, N_Q_HEADS, HEAD_DIM), qn_g), cos, sin)
    k = apply_rope(rms_norm(k.reshape(bsz, n, N_KV_HEADS, HEAD_DIM), kn_g), cos, sin)
    v = v.reshape(bsz, n, N_KV_HEADS, HEAD_DIM)
    nl = gm.shape[1]
    if ctx_out:
        a_c, q_c, k_c, v_c = jnp.split(gm @ w_in + b_in, splits, axis=-1)
    else:
        k_c, v_c = jnp.split(gm @ w_in[:, KV_COL:] + b_in[KV_COL:], 2, axis=-1)
    k_c = rms_norm(k_c.reshape(bsz, nl, N_KV_HEADS, HEAD_DIM), kn_g)
    v_c = v_c.reshape(bsz, nl, N_KV_HEADS, HEAD_DIM)
    att_h = windowed_attention(q, k, v, k_c, v_c, sink)
    out_h = jnp.concatenate([conv_h, att_h], axis=-1) @ w_out + b_out
    if not ctx_out:
        return out_h, None
    q_c = rms_norm(q_c.reshape(bsz, nl, N_Q_HEADS, HEAD_DIM), qn_g)
    att_c = context_attention(q_c, k_c, v_c, sink)
    conv_c = conformer_conv(a_c, conv_w, conv_b, ln_g, ln_b)
    out_c = jnp.concatenate([conv_c, att_c], axis=-1) @ w_out + b_out
    return out_h, out_c


def chunk_gmlp(xm, w_in, b_in, ln_g, ln_b, w_s, b_s, w_out, b_out):
    bsz, n, _ = xm.shape
    z = jax.nn.gelu(xm @ w_in + b_in, approximate=False)
    u, v = jnp.split(z, 2, axis=-1)
    v = layer_norm(v, ln_g, ln_b).reshape(bsz, n // CHUNK, CHUNK, GMLP_GROUPS, GMLP_GROUP_CH)
    sv = jnp.einsum('gpq,bnqgc->bnpgc', w_s, v) + b_s.T[:, :, None]
    return (u * sv.reshape(bsz, n, GMLP_CH)) @ w_out + b_out


def moe_ffn(xf, r_w, r_b, w1, b1, w2, b2):
    n, d = xf.shape
    logits = (xf @ r_w + r_b).astype(jnp.float32)
    top_val, top_idx = lax.top_k(logits, TOP_K)
    gates = jax.nn.softmax(top_val, axis=-1)
    n_assign = n * TOP_K
    flat_e = top_idx.reshape(-1)
    order = jnp.argsort(flat_e)
    e_sorted = flat_e[order]
    tok_sorted = order // TOP_K
    counts = jnp.zeros((N_EXPERTS,), jnp.int32).at[flat_e].add(1)
    padded = (counts + MOE_BLOCK - 1) // MOE_BLOCK * MOE_BLOCK
    start = jnp.cumsum(counts) - counts
    pend = jnp.cumsum(padded)
    pstart = pend - padded
    dest = pstart[e_sorted] + jnp.arange(n_assign) - start[e_sorted]
    n_blocks = (n_assign + N_EXPERTS * (MOE_BLOCK - 1) + MOE_BLOCK - 1) // MOE_BLOCK
    rows = jnp.zeros((n_blocks * MOE_BLOCK, d), xf.dtype).at[dest].set(xf[tok_sorted])
    block_expert = jnp.minimum(
        jnp.searchsorted(pend, jnp.arange(n_blocks) * MOE_BLOCK, side='right'), N_EXPERTS - 1)

    def expert_block(args):
        xb, e = args
        gate, up = jnp.split(xb @ w1[e] + b1[e], 2, axis=-1)
        gate = jnp.minimum(gate, SWIGLU_LIMIT)
        up = jnp.clip(up, -SWIGLU_LIMIT, SWIGLU_LIMIT)
        act = (up + 1.0) * gate * jax.nn.sigmoid(gate * SWIGLU_ALPHA)
        return act @ w2[e] + b2[e]

    y_rows = lax.map(expert_block, (rows.reshape(n_blocks, MOE_BLOCK, d), block_expert)).reshape(-1, d)
    y_assign = y_rows[dest] * gates.reshape(-1)[order][:, None].astype(xf.dtype)
    return jax.ops.segment_sum(y_assign, tok_sorted, num_segments=n)


def setup_inputs(seed: int = 0) -> dict:
    key = jax.random.key(seed)
    ks = iter(jax.random.split(key, 40))
    ne, no = (DEPTH + 1) // 2, DEPTH // 2
    d = D_MODEL

    def nrm(shape, scale):
        return jax.random.normal(next(ks), shape, jnp.float32) * scale

    return {
        'x': nrm((BATCH, SEQ, d), 1.0),
        'c': nrm((BATCH, d), 1.0),
        'ctx': nrm((BATCH, CTX_LEN, d), 1.0),
        'c_ctx': nrm((d,), 1.0),
        'ada_w': nrm((DEPTH, d, 6 * d), 0.5 * d ** -0.5),
        'ada_b': nrm((DEPTH, 6 * d), 0.02),
        'norm_mix_g': 1.0 + nrm((DEPTH, d), 0.1),
        'norm_ffn_g': 1.0 + nrm((DEPTH, d), 0.1),
        'ev_w_in': nrm((ne, d, EVEN_IN), d ** -0.5),
        'ev_b_in': nrm((ne, EVEN_IN), 0.02),
        'ev_conv_w': nrm((ne, CONV_WIDTH, CONV_CH), CONV_WIDTH ** -0.5),
        'ev_conv_b': nrm((ne, CONV_CH), 0.02),
        'ev_conv_ln_g': 1.0 + nrm((ne, CONV_CH), 0.1),
        'ev_conv_ln_b': nrm((ne, CONV_CH), 0.02),
        'ev_q_norm_g': 1.0 + nrm((ne, HEAD_DIM), 0.1),
        'ev_k_norm_g': 1.0 + nrm((ne, HEAD_DIM), 0.1),
        'ev_sink': nrm((ne, N_Q_HEADS), 0.5),
        'ev_w_out': nrm((ne, EVEN_MIX, d), EVEN_MIX ** -0.5),
        'ev_b_out': nrm((ne, d), 0.02),
        'od_w_in': nrm((no, d, 2 * GMLP_CH), d ** -0.5),
        'od_b_in': nrm((no, 2 * GMLP_CH), 0.02),
        'od_v_ln_g': 1.0 + nrm((no, GMLP_CH), 0.1),
        'od_v_ln_b': nrm((no, GMLP_CH), 0.02),
        'od_w_s': nrm((no, GMLP_GROUPS, CHUNK, CHUNK), CHUNK ** -0.5),
        'od_b_s': 1.0 + nrm((no, GMLP_GROUPS, CHUNK), 0.1),
        'od_w_out': nrm((no, GMLP_CH, d), GMLP_CH ** -0.5),
        'od_b_out': nrm((no, d), 0.02),
        'moe_router_w': nrm((DEPTH, d, N_EXPERTS), d ** -0.5),
        'moe_router_b': nrm((DEPTH, N_EXPERTS), 0.01),
        'moe_w1': nrm((DEPTH, N_EXPERTS, d, 2 * EXPERT_FF), d ** -0.5),
        'moe_b1': nrm((DEPTH, N_EXPERTS, 2 * EXPERT_FF), 0.02),
        'moe_w2': nrm((DEPTH, N_EXPERTS, EXPERT_FF, d), EXPERT_FF ** -0.5),
        'moe_b2': nrm((DEPTH, N_EXPERTS, d), 0.02),
    }


def reference(x, c, ctx, c_ctx, ada_w, ada_b, norm_mix_g, norm_ffn_g,
              ev_w_in, ev_b_in, ev_conv_w, ev_conv_b, ev_conv_ln_g, ev_conv_ln_b,
              ev_q_norm_g, ev_k_norm_g, ev_sink, ev_w_out, ev_b_out,
              od_w_in, od_b_in, od_v_ln_g, od_v_ln_b, od_w_s, od_b_s, od_w_out, od_b_out,
              moe_router_w, moe_router_b, moe_w1, moe_b1, moe_w2, moe_b2):
    n_lat = x.shape[1]
    rows = n_lat // GRID_W
    cos, sin = axial_rope(rows)
    h, g = x, ctx
    for l in range(DEPTH):
        i = l // 2
        is_even = l % 2 == 0
        ctx_read_later = any(j % 2 == 0 for j in range(l + 1, DEPTH))
        sh_m, sc_m, gt_m, sh_f, sc_f, gt_f = [m[:, None, :] for m in ada_modulation(c, ada_w[l], ada_b[l])]
        hm = modulate(h, norm_mix_g[l], sh_m, sc_m)
        if is_even or ctx_read_later:
            csh_m, csc_m, cgt_m, csh_f, csc_f, cgt_f = ada_modulation(c_ctx, ada_w[l], ada_b[l])
            gm = modulate(g, norm_mix_g[l], csh_m, csc_m)
        if is_even:
            mix_h, mix_g = even_mixer(hm, gm, ev_w_in[i], ev_b_in[i], ev_conv_w[i], ev_conv_b[i],
                                      ev_conv_ln_g[i], ev_conv_ln_b[i], ev_q_norm_g[i], ev_k_norm_g[i],
                                      ev_sink[i], ev_w_out[i], ev_b_out[i], cos, sin, ctx_read_later)
        else:
            gmlp_args = (od_w_in[i], od_b_in[i], od_v_ln_g[i], od_v_ln_b[i], od_w_s[i], od_b_s[i],
                         od_w_out[i], od_b_out[i])
            mix_h = chunk_gmlp(hm, *gmlp_args)
            mix_g = chunk_gmlp(gm, *gmlp_args) if ctx_read_later else None
        h = h + gt_m * mix_h
        moe_args = (moe_router_w[l], moe_router_b[l], moe_w1[l], moe_b1[l], moe_w2[l], moe_b2[l])
        hf = modulate(h, norm_ffn_g[l], sh_f, sc_f)
        if ctx_read_later:
            g = g + cgt_m * mix_g
            gf = modulate(g, norm_ffn_g[l], csh_f, csc_f)
            tokens = jnp.concatenate([gf, hf], axis=1)
            y = moe_ffn(tokens.reshape(-1, D_MODEL), *moe_args).reshape(tokens.shape)
            g = g + cgt_f * y[:, :g.shape[1]]
            h = h + gt_f * y[:, g.shape[1]:]
        else:
            h = h + gt_f * moe_ffn(hf.reshape(-1, D_MODEL), *moe_args).reshape(h.shape)
    return h
```

```python
import functools
import math

import jax
import jax.numpy as jnp
from jax import lax
from jax.experimental import pallas as pl
from jax.experimental.pallas import tpu as pltpu

F32 = jnp.float32
BF16 = jnp.bfloat16

D_MODEL = 1024
HEAD_DIM = 64
N_Q_HEADS = 8
N_KV_HEADS = 2
Q_DIM = N_Q_HEADS * HEAD_DIM
KV_DIM = N_KV_HEADS * HEAD_DIM
CONV_CH = 512
CONV_WIDTH = 31
CONV_HALO = 16
KV_COL = 2 * CONV_CH + Q_DIM
EVEN_IN = KV_COL + 2 * KV_DIM
ATT_BLOCK = 128
GRID_W = 64
ROPE_THETA = 10000.0
CHUNK = 128
GMLP_GROUPS = 8
N_EXPERTS = 32
TOP_K = 4
EXPERT_FF = 1024
SWIGLU_LIMIT = 7.0
SWIGLU_ALPHA = 1.702
EPS = 1e-6
NEG_INF = -1e30

TM = 512
EXPERT_BM = 256
MOD_ROWS = 16
VMEM_LIMIT = 48 * 1024 * 1024


def _cparams(sem):
    return pltpu.CompilerParams(dimension_semantics=sem, vmem_limit_bytes=VMEM_LIMIT)


def _sigmoid(x):
    return 1.0 / (1.0 + jnp.exp(-x))


def _bdot(a, b):
    return jnp.dot(a.astype(BF16), b.astype(BF16), preferred_element_type=F32)


def _ada_kernel(cc_ref, w_ref, b_ref, o_ref):
    a = cc_ref[...]
    o_ref[0] = _bdot(a * _sigmoid(a), w_ref[0]) + b_ref[0]


def _ada(cc, ada_w, ada_b):
    depth, d, n = ada_w.shape
    tn = 1536
    return pl.pallas_call(
        _ada_kernel,
        grid=(depth, n // tn),
        in_specs=[
            pl.BlockSpec((MOD_ROWS, d), lambda l, j: (0, 0)),
            pl.BlockSpec((1, d, tn), lambda l, j: (l, 0, j)),
            pl.BlockSpec((1, 1, tn), lambda l, j: (l, 0, j)),
        ],
        out_specs=pl.BlockSpec((1, MOD_ROWS, tn), lambda l, j: (l, 0, j)),
        out_shape=jax.ShapeDtypeStruct((depth, MOD_ROWS, n), F32),
        compiler_params=_cparams(("arbitrary", "arbitrary")),
        name="ada",
    )(cc, ada_w, ada_b.reshape(depth, 1, n))


def _mod_row(mod_ref, row, part):
    return mod_ref[pl.ds(row, 1), pl.ds(part * D_MODEL, D_MODEL)]


def _modulate(x, g, shift, scale):
    ms = jnp.mean(x * x, axis=-1, keepdims=True)
    return x * lax.rsqrt(ms + EPS) * g * (1.0 + scale) + shift


SLABS = D_MODEL // 128


def _store_slabs(ref, value):
    n = value.shape[0]
    for s in range(SLABS):
        ref[pl.ds(s, n, stride=SLABS), :] = value[:, 128 * s:128 * (s + 1)]


def _load_slabs(ref, n):
    return jnp.concatenate([ref[pl.ds(s, n, stride=SLABS), :] for s in range(SLABS)], axis=1)


def _head_rms(t, blockdiag, g):
    t2 = t * t
    hi = t2.astype(BF16)
    lo = (t2 - hi.astype(F32)).astype(BF16)
    ms = (jnp.dot(hi, blockdiag, preferred_element_type=F32)
          + jnp.dot(lo, blockdiag, preferred_element_type=F32))
    return t * lax.rsqrt(ms + EPS) * g


def _rope(t, cos, sin_signed):
    width = t.shape[-1]
    lane = lax.broadcasted_iota(jnp.int32, t.shape, 1)
    first = (lane % HEAD_DIM) < (HEAD_DIM // 2)
    partner = jnp.where(first, pltpu.roll(t, width - HEAD_DIM // 2, 1), pltpu.roll(t, HEAD_DIM // 2, 1))
    return t * cos + partner * sin_signed


def _inproj_kernel(x_ref, mod_ref, g_ref, w_ref, b_ref, cos_ref, sin_ref, qg_ref, kg_ref, bq_ref, bk_ref,
                   u_ref, qa_ref, qb_ref, k_ref, v_ref, *, tiles_per_batch):
    row = pl.program_id(0) // tiles_per_batch
    hm = _modulate(x_ref[...], g_ref[...], _mod_row(mod_ref, row, 0), _mod_row(mod_ref, row, 1))
    y = _bdot(hm, w_ref[...]) + b_ref[...]
    u_ref[...] = y[:, :CONV_CH] * _sigmoid(y[:, CONV_CH:2 * CONV_CH])
    cos = cos_ref[...]
    sin = sin_ref[...]
    q = _head_rms(y[:, 2 * CONV_CH:KV_COL], bq_ref[...], qg_ref[...])
    q = _rope(q, cos, sin) * (HEAD_DIM ** -0.5)
    lane = lax.broadcasted_iota(jnp.int32, q.shape, 1)
    low = (lane % (2 * HEAD_DIM)) < HEAD_DIM
    qa_ref[...] = jnp.where(low, q, 0.0).astype(BF16)
    qb_ref[...] = jnp.where(low, 0.0, q).astype(BF16)
    k = _head_rms(y[:, KV_COL:KV_COL + KV_DIM], bk_ref[...], kg_ref[...])
    k_ref[...] = _rope(k, cos[:, :KV_DIM], sin[:, :KV_DIM]).astype(BF16)
    v_ref[...] = y[:, KV_COL + KV_DIM:].astype(BF16)


def _inproj(x2, mod, g, w_bf, b, cos_t, sin_t, qg, kg, bq, bk, seq):
    t, d = x2.shape
    tpb = seq // TM
    const = lambda i: (0, 0)
    tile = lambda i: (i, 0)
    pos = lambda i: (i % tpb, 0)
    return pl.pallas_call(
        functools.partial(_inproj_kernel, tiles_per_batch=tpb),
        grid=(t // TM,),
        in_specs=[
            pl.BlockSpec((TM, d), tile),
            pl.BlockSpec(mod.shape, const),
            pl.BlockSpec((1, d), const),
            pl.BlockSpec(w_bf.shape, const),
            pl.BlockSpec((1, EVEN_IN), const),
            pl.BlockSpec((TM, Q_DIM), pos),
            pl.BlockSpec((TM, Q_DIM), pos),
            pl.BlockSpec((1, Q_DIM), const),
            pl.BlockSpec((1, KV_DIM), const),
            pl.BlockSpec((Q_DIM, Q_DIM), const),
            pl.BlockSpec((KV_DIM, KV_DIM), const),
        ],
        out_specs=[
            pl.BlockSpec((TM, CONV_CH), tile),
            pl.BlockSpec((TM, Q_DIM), tile),
            pl.BlockSpec((TM, Q_DIM), tile),
            pl.BlockSpec((TM, KV_DIM), tile),
            pl.BlockSpec((TM, KV_DIM), tile),
        ],
        out_shape=[
            jax.ShapeDtypeStruct((t, CONV_CH), F32),
            jax.ShapeDtypeStruct((t, Q_DIM), BF16),
            jax.ShapeDtypeStruct((t, Q_DIM), BF16),
            jax.ShapeDtypeStruct((t, KV_DIM), BF16),
            jax.ShapeDtypeStruct((t, KV_DIM), BF16),
        ],
        compiler_params=_cparams(("arbitrary",)),
        name="inproj",
    )(x2, mod, g, w_bf, b, cos_t, sin_t, qg, kg, bq, bk)


def _ctx_kv_kernel(g_in_ref, mod_ref, g_ref, w_ref, b_ref, kg_ref, bk_ref, k_ref, v_ref, *, mod_row):
    gm = _modulate(g_in_ref[...], g_ref[...], _mod_row(mod_ref, mod_row, 0), _mod_row(mod_ref, mod_row, 1))
    y = _bdot(gm, w_ref[...]) + b_ref[...]
    k_ref[...] = _head_rms(y[:, :KV_DIM], bk_ref[...], kg_ref[...]).astype(BF16)
    v_ref[...] = y[:, KV_DIM:].astype(BF16)


def _ctx_kv(ctx2, mod, g, w_kv_bf, b_kv, kg, bk, mod_row):
    t, d = ctx2.shape
    const = lambda i: (0, 0)
    tile = lambda i: (i, 0)
    return pl.pallas_call(
        functools.partial(_ctx_kv_kernel, mod_row=mod_row),
        grid=(t // TM,),
        in_specs=[
            pl.BlockSpec((TM, d), tile),
            pl.BlockSpec(mod.shape, const),
            pl.BlockSpec((1, d), const),
            pl.BlockSpec(w_kv_bf.shape, const),
            pl.BlockSpec((1, 2 * KV_DIM), const),
            pl.BlockSpec((1, KV_DIM), const),
            pl.BlockSpec((KV_DIM, KV_DIM), const),
        ],
        out_specs=[pl.BlockSpec((TM, KV_DIM), tile), pl.BlockSpec((TM, KV_DIM), tile)],
        out_shape=[jax.ShapeDtypeStruct((t, KV_DIM), BF16), jax.ShapeDtypeStruct((t, KV_DIM), BF16)],
        compiler_params=_cparams(("arbitrary",)),
        name="ctx_kv",
    )(ctx2, mod, g, w_kv_bf, b_kv, kg, bk)


CONV_ROWS = 64


def _conv_kernel(u_ref, w_ref, cb_ref, lg_ref, lb_ref, o_ref, pad_ref):
    seq = u_ref.shape[1]
    zeros = jnp.zeros((CONV_HALO, CONV_CH), F32)
    pad_ref[pl.ds(0, CONV_HALO), :] = zeros
    pad_ref[pl.ds(CONV_HALO, seq), :] = u_ref[0]
    pad_ref[pl.ds(CONV_HALO + seq, CONV_HALO), :] = zeros
    first = CONV_HALO - CONV_WIDTH // 2

    def body(i, carry):
        t0 = pl.multiple_of(i * CONV_ROWS, CONV_ROWS)
        win = pad_ref[pl.ds(t0, CONV_ROWS + 2 * CONV_HALO), :]
        acc = jnp.zeros((CONV_ROWS, CONV_CH), F32) + cb_ref[...]
        for tap in range(CONV_WIDTH):
            acc = acc + w_ref[tap:tap + 1, :] * win[first + tap:first + tap + CONV_ROWS, :]
        mu = jnp.mean(acc, axis=-1, keepdims=True)
        cen = acc - mu
        var = jnp.mean(cen * cen, axis=-1, keepdims=True)
        yn = cen * lax.rsqrt(var + EPS) * lg_ref[...] + lb_ref[...]
        o_ref[0, pl.ds(t0, CONV_ROWS), :] = (yn * _sigmoid(yn)).astype(BF16)
        return carry

    lax.fori_loop(0, seq // CONV_ROWS, body, 0)


def _conv(u3, w_pad, cb, lg, lb):
    bsz, seq, ch = u3.shape
    const = lambda b: (0, 0)
    return pl.pallas_call(
        _conv_kernel,
        grid=(bsz,),
        in_specs=[
            pl.BlockSpec((1, seq, ch), lambda b: (b, 0, 0)),
            pl.BlockSpec(w_pad.shape, const),
            pl.BlockSpec((1, ch), const),
            pl.BlockSpec((1, ch), const),
            pl.BlockSpec((1, ch), const),
        ],
        out_specs=pl.BlockSpec((1, seq, ch), lambda b: (b, 0, 0)),
        out_shape=jax.ShapeDtypeStruct((bsz, seq, ch), BF16),
        scratch_shapes=[pltpu.VMEM((seq + 2 * CONV_HALO, ch), F32)],
        compiler_params=_cparams(("arbitrary",)),
        name="conv",
    )(u3, w_pad, cb, lg, lb)


def _attn_kernel(sink_ref, qa_ref, qb_ref, kp_ref, kc_ref, kn_ref, kx_ref, vp_ref, vc_ref, vn_ref, vx_ref, o_ref):
    n = pl.program_id(1)
    nb = pl.num_programs(1)
    kcat = jnp.concatenate([kp_ref[0], kc_ref[0], kn_ref[0], kx_ref[0]], axis=0)
    vcat = jnp.concatenate([vp_ref[0], vc_ref[0], vn_ref[0], vx_ref[0]], axis=0)
    blk = ATT_BLOCK
    qi = lax.broadcasted_iota(jnp.int32, (blk, blk), 0)
    kj = lax.broadcasted_iota(jnp.int32, (blk, blk), 1)
    mask_prev = (kj >= qi) & (n > 0)
    mask_next = (kj <= qi) & (n < nb - 1)
    n_chunks = Q_DIM // (2 * HEAD_DIM)
    res = []
    for kvh, q_ref in enumerate((qa_ref, qb_ref)):
        qv = q_ref[0]
        q4 = jnp.concatenate([qv[:, 128 * j:128 * (j + 1)] for j in range(n_chunks)], axis=0)
        s = lax.dot_general(q4, kcat, (((1,), (1,)), ((), ())), preferred_element_type=F32)
        outs = []
        for j in range(n_chunks):
            sj = s[blk * j:blk * (j + 1)]
            sp = jnp.where(mask_prev, sj[:, :blk], NEG_INF)
            sc = sj[:, blk:2 * blk]
            sn = jnp.where(mask_next, sj[:, 2 * blk:3 * blk], NEG_INF)
            sx = sj[:, 3 * blk:]
            sink = sink_ref[kvh * n_chunks + j]
            m = jnp.maximum(jnp.maximum(jnp.max(sp, axis=-1, keepdims=True), jnp.max(sc, axis=-1, keepdims=True)),
                            jnp.maximum(jnp.max(sn, axis=-1, keepdims=True), jnp.max(sx, axis=-1, keepdims=True)))
            m = jnp.maximum(m, sink)
            parts = [jnp.exp(t - m) for t in (sp, sc, sn, sx)]
            den = sum(jnp.sum(p, axis=-1, keepdims=True) for p in parts) + jnp.exp(sink - m)
            pcat = jnp.concatenate(parts, axis=-1).astype(BF16)
            o = jnp.dot(pcat, vcat, preferred_element_type=F32)
            outs.append(o * (1.0 / den))
        res.append(outs)
    lane = lax.broadcasted_iota(jnp.int32, (blk, 2 * HEAD_DIM), 1)
    low = lane < HEAD_DIM
    o_ref[0] = jnp.concatenate([jnp.where(low, res[0][j], res[1][j]) for j in range(n_chunks)],
                               axis=-1).astype(BF16)


def _attention(sink_perm, qa, qb, k, v, kx, vx):
    bsz, seq, _ = qa.shape
    nb = seq // ATT_BLOCK
    ctx_len = kx.shape[1]
    cur = lambda b, n, s: (b, n, 0)
    prev = lambda b, n, s: (b, jnp.maximum(n - 1, 0), 0)
    nxt = lambda b, n, s: (b, jnp.minimum(n + 1, nb - 1), 0)
    whole = lambda b, n, s: (b, 0, 0)
    kvspec = lambda im: pl.BlockSpec((1, ATT_BLOCK, KV_DIM), im)
    return pl.pallas_call(
        _attn_kernel,
        grid_spec=pltpu.PrefetchScalarGridSpec(
            num_scalar_prefetch=1,
            grid=(bsz, nb),
            in_specs=[
                pl.BlockSpec((1, ATT_BLOCK, Q_DIM), cur),
                pl.BlockSpec((1, ATT_BLOCK, Q_DIM), cur),
                kvspec(prev), kvspec(cur), kvspec(nxt),
                pl.BlockSpec((1, ctx_len, KV_DIM), whole),
                kvspec(prev), kvspec(cur), kvspec(nxt),
                pl.BlockSpec((1, ctx_len, KV_DIM), whole),
            ],
            out_specs=pl.BlockSpec((1, ATT_BLOCK, Q_DIM), cur),
        ),
        out_shape=jax.ShapeDtypeStruct((bsz, seq, Q_DIM), BF16),
        compiler_params=_cparams(("arbitrary", "arbitrary")),
        name="attention",
    )(sink_perm, qa, qb, k, k, k, kx, v, v, v, vx)


def _erf(x):
    return lax.erf(x)


def _gmlp_kernel(x_ref, mod_ref, g_ref, w_ref, b_ref, lg_ref, lb_ref, ws_ref, bst_ref, a_ref, *, tiles_per_batch):
    row = pl.program_id(0) // tiles_per_batch
    hm = _modulate(x_ref[...], g_ref[...], _mod_row(mod_ref, row, 0), _mod_row(mod_ref, row, 1))
    z = _bdot(hm, w_ref[...]) + b_ref[...]
    z = 0.5 * z * (1.0 + _erf(z * (1.0 / math.sqrt(2.0))))
    half = z.shape[1] // 2
    u = z[:, :half]
    v = z[:, half:]
    mu = jnp.mean(v, axis=-1, keepdims=True)
    cen = v - mu
    var = jnp.mean(cen * cen, axis=-1, keepdims=True)
    vn = (cen * lax.rsqrt(var + EPS) * lg_ref[...] + lb_ref[...]).astype(BF16)
    gch = half // GMLP_GROUPS
    for c in range(x_ref.shape[0] // CHUNK):
        rows = slice(c * CHUNK, (c + 1) * CHUNK)
        for g in range(GMLP_GROUPS):
            cols = slice(g * gch, (g + 1) * gch)
            sv = jnp.dot(ws_ref[g], vn[rows, cols], preferred_element_type=F32) + bst_ref[:, g:g + 1]
            a_ref[rows, cols] = (u[rows, cols] * sv).astype(BF16)


def _gmlp(x2, mod, g, w_bf, b, lg, lb, ws_bf, bst, seq):
    t, d = x2.shape
    tpb = seq // TM
    const = lambda i: (0, 0)
    tile = lambda i: (i, 0)
    n_in = w_bf.shape[1]
    return pl.pallas_call(
        functools.partial(_gmlp_kernel, tiles_per_batch=tpb),
        grid=(t // TM,),
        in_specs=[
            pl.BlockSpec((TM, d), tile),
            pl.BlockSpec(mod.shape, const),
            pl.BlockSpec((1, d), const),
            pl.BlockSpec(w_bf.shape, const),
            pl.BlockSpec((1, n_in), const),
            pl.BlockSpec((1, n_in // 2), const),
            pl.BlockSpec((1, n_in // 2), const),
            pl.BlockSpec(ws_bf.shape, lambda i: (0, 0, 0)),
            pl.BlockSpec(bst.shape, const),
        ],
        out_specs=pl.BlockSpec((TM, n_in // 2), tile),
        out_shape=jax.ShapeDtypeStruct((t, n_in // 2), BF16),
        compiler_params=_cparams(("arbitrary",)),
        name="gmlp",
    )(x2, mod, g, w_bf, b, lg, lb, ws_bf, bst)


def _post_kernel(*refs, n_parts, tiles_per_batch):
    a_refs = refs[:n_parts]
    w_refs = refs[n_parts:2 * n_parts]
    bo_ref, x_ref, mod_ref, gf_ref, rwt_ref, rb_ref, h1_ref, hf_ref, idx_ref, gate_ref = refs[2 * n_parts:]
    row = pl.program_id(0) // tiles_per_batch
    mix = bo_ref[...]
    for a_ref, w_ref in zip(a_refs, w_refs):
        mix = mix + jnp.dot(a_ref[...], w_ref[...], preferred_element_type=F32)
    h1 = x_ref[...] + _mod_row(mod_ref, row, 2) * mix
    h1_ref[...] = h1
    hf = _modulate(h1, gf_ref[...], _mod_row(mod_ref, row, 3), _mod_row(mod_ref, row, 4))
    _store_slabs(hf_ref, hf)
    logits = lax.dot_general(rwt_ref[...], hf.astype(BF16), (((1,), (1,)), ((), ())),
                             preferred_element_type=F32) + rb_ref[...]
    eid = lax.broadcasted_iota(jnp.int32, logits.shape, 0)
    vals, idxs = [], []
    for _ in range(TOP_K):
        m = jnp.max(logits, axis=0, keepdims=True)
        pick = jnp.min(jnp.where(logits == m, eid, N_EXPERTS), axis=0, keepdims=True)
        vals.append(m)
        idxs.append(pick)
        logits = jnp.where(eid == pick, -jnp.inf, logits)
    exps = [jnp.exp(v - vals[0]) for v in vals]
    inv = 1.0 / sum(exps)
    tm = logits.shape[1]
    idx_ref[0] = jnp.concatenate(idxs + [jnp.zeros((8 - TOP_K, tm), jnp.int32)], axis=0)
    gate_ref[0] = jnp.concatenate([e * inv for e in exps] + [jnp.zeros((8 - TOP_K, tm), F32)], axis=0)


def _post(a_parts, w_parts, b_out, x2, mod, gf, rwt_bf, rb_col, seq):
    t, d = x2.shape
    tpb = seq // TM
    nt = t // TM
    const = lambda i: (0, 0)
    tile = lambda i: (i, 0)
    in_specs = ([pl.BlockSpec((TM, a.shape[1]), tile) for a in a_parts]
                + [pl.BlockSpec(w.shape, const) for w in w_parts]
                + [pl.BlockSpec((1, d), const), pl.BlockSpec((TM, d), tile), pl.BlockSpec(mod.shape, const),
                   pl.BlockSpec((1, d), const), pl.BlockSpec(rwt_bf.shape, const),
                   pl.BlockSpec(rb_col.shape, const)])
    return pl.pallas_call(
        functools.partial(_post_kernel, n_parts=len(a_parts), tiles_per_batch=tpb),
        grid=(nt,),
        in_specs=in_specs,
        out_specs=[
            pl.BlockSpec((TM, d), tile),
            pl.BlockSpec((TM * SLABS, 128), tile),
            pl.BlockSpec((1, 8, TM), lambda i: (i, 0, 0)),
            pl.BlockSpec((1, 8, TM), lambda i: (i, 0, 0)),
        ],
        out_shape=[
            jax.ShapeDtypeStruct((t, d), F32),
            jax.ShapeDtypeStruct((t * SLABS, 128), F32),
            jax.ShapeDtypeStruct((nt, 8, TM), jnp.int32),
            jax.ShapeDtypeStruct((nt, 8, TM), F32),
        ],
        compiler_params=_cparams(("arbitrary",)),
        name="post",
    )(*a_parts, *w_parts, b_out, x2, mod, gf, rwt_bf, rb_col)


META_LANES = 384


def _max_blocks(n_tokens):
    return (n_tokens * TOP_K + N_EXPERTS * (EXPERT_BM - 1) + EXPERT_BM - 1) // EXPERT_BM


def _route_kernel(idx_ref, dest_ref, meta_ref, pfull_ref):
    nt, _, tc = idx_ref.shape
    e = N_EXPERTS
    eid = lax.broadcasted_iota(jnp.int32, (e, tc), 0)
    r = lax.broadcasted_iota(jnp.int32, (tc, tc), 0)
    c = lax.broadcasted_iota(jnp.int32, (tc, tc), 1)
    upper = (r < c).astype(BF16)

    def onehots(ci):
        blk = idx_ref[ci]
        return [eid == blk[k:k + 1, :] for k in range(TOP_K)]

    def count_body(ci, carry):
        oh = onehots(ci)
        member = sum(o.astype(F32) for o in oh)
        pfull_ref[ci] = jnp.dot(member.astype(BF16), upper, preferred_element_type=F32) + carry
        return carry + jnp.sum(member, axis=1, keepdims=True)

    counts = lax.fori_loop(0, nt, count_body, jnp.zeros((e, 1), F32))
    nblk = jnp.floor((counts + (EXPERT_BM - 1)) * (1.0 / EXPERT_BM))
    rr = lax.broadcasted_iota(jnp.int32, (e, e), 0)
    cc = lax.broadcasted_iota(jnp.int32, (e, e), 1)
    lower = (cc <= rr).astype(BF16)
    pend_b = jnp.dot(lower, jnp.broadcast_to(nblk, (e, 128)).astype(BF16),
                     preferred_element_type=F32)[:, 0:1]
    pstart = (pend_b - nblk) * EXPERT_BM

    def dest_body(ci, carry):
        oh = onehots(ci)
        base = pfull_ref[ci] + pstart
        rows = [jnp.sum(jnp.where(o, base, 0.0), axis=0, keepdims=True) for o in oh]
        dest_ref[ci] = jnp.concatenate(rows + [jnp.zeros((8 - TOP_K, tc), F32)], axis=0).astype(jnp.int32)
        return carry

    lax.fori_loop(0, nt, dest_body, 0)
    lanes = meta_ref.shape[1]
    bid = lax.broadcasted_iota(jnp.int32, (e, lanes), 1)
    sub = lax.broadcasted_iota(jnp.int32, (e, lanes), 0)
    block_expert = jnp.minimum(jnp.sum((pend_b <= bid.astype(F32)).astype(F32), axis=0, keepdims=True),
                               float(e - 1))
    n_used = jnp.broadcast_to(pend_b[e - 1:e, :], (1, lanes))
    to_lanes = lambda col: jnp.sum(jnp.where(sub == bid, col, 0.0), axis=0, keepdims=True)
    last_block = to_lanes(jnp.maximum(pend_b - 1.0, 0.0))
    meta_ref[...] = jnp.concatenate([block_expert, n_used, last_block, to_lanes(nblk),
                                     jnp.zeros((4, lanes), F32)], axis=0).astype(jnp.int32)


def _route(idx3):
    nt, _, tc = idx3.shape
    return pl.pallas_call(
        _route_kernel,
        out_shape=[jax.ShapeDtypeStruct((nt, 8, tc), jnp.int32),
                   jax.ShapeDtypeStruct((8, META_LANES), jnp.int32)],
        scratch_shapes=[pltpu.VMEM((nt, N_EXPERTS, tc), F32)],
        compiler_params=pltpu.CompilerParams(vmem_limit_bytes=VMEM_LIMIT),
        name="route",
    )(idx3)


def _dispatch_kernel(last_block_ref, nblk_ref, nu_ref, dest_ref, hf_ref, rows_ref, zbuf_ref, sem):
    tm = dest_ref.shape[2]
    step = pl.program_id(0)

    @pl.when(step == 0)
    def _():
        zbuf_ref[...] = jnp.zeros(zbuf_ref.shape, zbuf_ref.dtype)

        def fill(block):
            return pltpu.make_async_copy(zbuf_ref, rows_ref.at[pl.ds(block * EXPERT_BM, EXPERT_BM)], sem)

        for ex in range(N_EXPERTS):
            pl.when(nblk_ref[ex] > 0)(lambda ex=ex: fill(last_block_ref[ex]).start())
        n_total = rows_ref.shape[0] // EXPERT_BM

        def tail_start(b, carry):
            fill(b).start()
            return carry

        def tail_wait(b, carry):
            fill(b).wait()
            return carry

        lax.fori_loop(nu_ref[0], n_total, tail_start, 0)
        for ex in range(N_EXPERTS):
            pl.when(nblk_ref[ex] > 0)(lambda ex=ex: fill(last_block_ref[ex]).wait())
        lax.fori_loop(nu_ref[0], n_total, tail_wait, 0)

    def row_copy(j, k):
        return pltpu.make_async_copy(hf_ref.at[step * tm + j], rows_ref.at[dest_ref[0, k, j]], sem)

    def start_body(j, carry):
        for k in range(TOP_K):
            row_copy(j, k).start()
        return carry

    def wait_body(j, carry):
        for k in range(TOP_K):
            row_copy(j, k).wait()
        return carry

    lax.fori_loop(0, tm, start_body, 0)
    lax.fori_loop(0, tm, wait_body, 0)


def _dispatch(last_block, nblk, n_used, dest3, hf3):
    t = hf3.shape[0]
    n_rows = _max_blocks(t) * EXPERT_BM
    return pl.pallas_call(
        _dispatch_kernel,
        grid_spec=pltpu.PrefetchScalarGridSpec(
            num_scalar_prefetch=3,
            grid=(t // TM,),
            in_specs=[
                pl.BlockSpec((1, 8, TM), lambda i, lb, nb, nu: (i, 0, 0), memory_space=pltpu.SMEM),
                pl.BlockSpec(memory_space=pl.ANY),
            ],
            out_specs=pl.BlockSpec(memory_space=pl.ANY),
            scratch_shapes=[pltpu.VMEM((EXPERT_BM, SLABS, 128), hf3.dtype), pltpu.SemaphoreType.DMA],
        ),
        out_shape=jax.ShapeDtypeStruct((n_rows, SLABS, 128), hf3.dtype),
        compiler_params=_cparams(("arbitrary",)),
        name="dispatch",
    )(last_block, nblk, n_used, dest3, hf3)


def _expert_kernel(be_ref, nu_ref, x_ref, w1_ref, b1_ref, w2_ref, b2_ref, y_ref, w1b_ref, w2b_ref):
    b = pl.program_id(0)

    @pl.when(b >= nu_ref[0])
    def _():
        y_ref[...] = jnp.zeros(y_ref.shape, y_ref.dtype)

    @pl.when(b < nu_ref[0])
    def _():
        prev = be_ref[jnp.maximum(b - 1, 0)]

        @pl.when((b == 0) | (be_ref[b] != prev))
        def _():
            w1b_ref[...] = w1_ref[0].astype(BF16)
            w2b_ref[...] = w2_ref[0].astype(BF16)

        x = _load_slabs(x_ref, EXPERT_BM)
        h = jnp.dot(x.astype(BF16), w1b_ref[...], preferred_element_type=F32) + b1_ref[0]
        gate = jnp.minimum(h[:, :EXPERT_FF], SWIGLU_LIMIT)
        up = jnp.clip(h[:, EXPERT_FF:], -SWIGLU_LIMIT, SWIGLU_LIMIT)
        act = (up + 1.0) * gate * _sigmoid(gate * SWIGLU_ALPHA)
        _store_slabs(y_ref, jnp.dot(act.astype(BF16), w2b_ref[...], preferred_element_type=F32) + b2_ref[0])


def _experts(block_expert, n_used, rows2, w1, b1, w2, b2):
    n_blocks = rows2.shape[0] // (EXPERT_BM * SLABS)
    _, d, ff2 = w1.shape
    blk = lambda b, be, nu: (jnp.minimum(b, nu[0] - 1), 0)
    exp3 = lambda b, be, nu: (be[jnp.minimum(b, nu[0] - 1)], 0, 0)
    return pl.pallas_call(
        _expert_kernel,
        grid_spec=pltpu.PrefetchScalarGridSpec(
            num_scalar_prefetch=2,
            grid=(n_blocks,),
            in_specs=[
                pl.BlockSpec((EXPERT_BM * SLABS, 128), blk),
                pl.BlockSpec((1, d, ff2), exp3),
                pl.BlockSpec((1, 1, ff2), exp3),
                pl.BlockSpec((1, ff2 // 2, d), exp3),
                pl.BlockSpec((1, 1, d), exp3),
            ],
            out_specs=pl.BlockSpec((EXPERT_BM * SLABS, 128), lambda b, be, nu: (b, 0)),
            scratch_shapes=[pltpu.VMEM((d, ff2), BF16), pltpu.VMEM((ff2 // 2, d), BF16)],
        ),
        out_shape=jax.ShapeDtypeStruct(rows2.shape, F32),
        compiler_params=_cparams(("arbitrary",)),
        name="experts",
    )(block_expert, n_used, rows2, w1, b1.reshape(N_EXPERTS, 1, ff2), w2, b2.reshape(N_EXPERTS, 1, d))


def _combine_kernel(dest_ref, gate_ref, h1_ref, mod_ref, y_ref, o_ref, ybuf_ref, sem, *, tiles_per_batch):
    tm = h1_ref.shape[0]
    row = pl.program_id(0) // tiles_per_batch

    def row_copy(j, k):
        return pltpu.make_async_copy(y_ref.at[dest_ref[0, k, j]],
                                     ybuf_ref.at[k, pl.ds(pl.multiple_of(j * SLABS, SLABS), SLABS)], sem)

    def start_body(j, carry):
        for k in range(TOP_K):
            row_copy(j, k).start()
        return carry

    def wait_body(j, carry):
        for k in range(TOP_K):
            row_copy(j, k).wait()
        return carry

    lax.fori_loop(0, tm, start_body, 0)
    lax.fori_loop(0, tm, wait_body, 0)
    gates = gate_ref[0].T
    acc = gates[:, 0:1] * _load_slabs(ybuf_ref.at[0], tm)
    for k in range(1, TOP_K):
        acc = acc + gates[:, k:k + 1] * _load_slabs(ybuf_ref.at[k], tm)
    o_ref[...] = h1_ref[...] + _mod_row(mod_ref, row, 5) * acc


def _combine(dest3, gates3, h1, mod, y_rows, seq):
    t, d = h1.shape
    tpb = seq // TM
    return pl.pallas_call(
        functools.partial(_combine_kernel, tiles_per_batch=tpb),
        grid=(t // TM,),
        in_specs=[
            pl.BlockSpec((1, 8, TM), lambda i: (i, 0, 0), memory_space=pltpu.SMEM),
            pl.BlockSpec((1, 8, TM), lambda i: (i, 0, 0)),
            pl.BlockSpec((TM, d), lambda i: (i, 0)),
            pl.BlockSpec(mod.shape, lambda i: (0, 0)),
            pl.BlockSpec(memory_space=pl.ANY),
        ],
        out_specs=pl.BlockSpec((TM, d), lambda i: (i, 0)),
        out_shape=jax.ShapeDtypeStruct((t, d), F32),
        scratch_shapes=[pltpu.VMEM((TOP_K, TM * SLABS, 128), F32), pltpu.SemaphoreType.DMA],
        compiler_params=_cparams(("arbitrary",)),
        name="combine",
    )(dest3, gates3, h1, mod, y_rows)


def _moe(h1, hf2, idx3, gates3, mod, w1, b1, w2, b2, seq):
    t = h1.shape[0]
    dest3, meta = _route(idx3)
    n_blocks = _max_blocks(t)
    n_used = meta[1, :1]
    rows3 = _dispatch(meta[2, :N_EXPERTS], meta[3, :N_EXPERTS], n_used, dest3, hf2.reshape(t, SLABS, 128))
    y2 = _experts(meta[0, :n_blocks], n_used, rows3.reshape(-1, 128), w1, b1, w2, b2)
    return _combine(dest3, gates3, h1, mod, y2.reshape(-1, SLABS, 128), seq)


def _rope_tables(seq):
    rows = seq // GRID_W
    r = jnp.repeat(jnp.arange(rows), GRID_W).astype(F32)
    col = jnp.tile(jnp.arange(GRID_W), rows).astype(F32)
    pairs = HEAD_DIM // 4
    inv_freq = ROPE_THETA ** (-jnp.arange(pairs, dtype=F32) / pairs)
    ang = jnp.concatenate([r[:, None] * inv_freq, col[:, None] * inv_freq], axis=-1)
    cos, sin = jnp.cos(ang), jnp.sin(ang)
    cos_t = jnp.tile(cos, (1, Q_DIM // (HEAD_DIM // 2)))
    sin_t = jnp.tile(jnp.concatenate([-sin, sin], axis=-1), (1, N_Q_HEADS))
    return cos_t, sin_t


def _block_diag_mean(width):
    i = jnp.arange(width)
    return jnp.where((i[:, None] // HEAD_DIM) == (i[None, :] // HEAD_DIM), 1.0 / HEAD_DIM, 0.0).astype(BF16)


def _head_perm():
    per_kv = N_Q_HEADS // N_KV_HEADS
    heads = []
    for j in range(per_kv):
        heads += [j, per_kv + j]
    return jnp.concatenate([jnp.arange(HEAD_DIM) + h * HEAD_DIM for h in heads])


def kernel(x, c, ctx, c_ctx, ada_w, ada_b, norm_mix_g, norm_ffn_g, ev_w_in, ev_b_in, ev_conv_w, ev_conv_b, ev_conv_ln_g, ev_conv_ln_b, ev_q_norm_g, ev_k_norm_g, ev_sink, ev_w_out, ev_b_out, od_w_in, od_b_in, od_v_ln_g, od_v_ln_b, od_w_s, od_b_s, od_w_out, od_b_out, moe_router_w, moe_router_b, moe_w1, moe_b1, moe_w2, moe_b2):
    bsz, seq, d = x.shape
    ctx_len = ctx.shape[1]
    assert d == D_MODEL and seq % TM == 0 and (bsz * ctx_len) % TM == 0 and bsz < MOD_ROWS
    assert ada_w.shape[0] == 2, "layer 0 is the conv/attention layer, layer 1 the gMLP layer"
    ctx_row = bsz
    cc = jnp.zeros((MOD_ROWS, d), F32).at[:bsz].set(c).at[ctx_row].set(c_ctx)
    mods = _ada(cc, ada_w, ada_b)
    row2 = lambda v: v.reshape(1, -1)

    perm = _head_perm()
    w_in = ev_w_in[0]
    w_in_bf = jnp.concatenate([w_in[:, :2 * CONV_CH], w_in[:, 2 * CONV_CH:KV_COL][:, perm], w_in[:, KV_COL:]],
                              axis=1).astype(BF16)
    b_in = ev_b_in[0]
    b_in_p = row2(jnp.concatenate([b_in[:2 * CONV_CH], b_in[2 * CONV_CH:KV_COL][perm], b_in[KV_COL:]]))
    cos_t, sin_t = _rope_tables(seq)
    qg = row2(jnp.tile(ev_q_norm_g[0], N_Q_HEADS))
    kg = row2(jnp.tile(ev_k_norm_g[0], N_KV_HEADS))
    bq, bk = _block_diag_mean(Q_DIM), _block_diag_mean(KV_DIM)
    x2 = x.reshape(bsz * seq, d)
    g_mix0 = row2(norm_mix_g[0])
    u, qa, qb, k, v = _inproj(x2, mods[0], g_mix0, w_in_bf, b_in_p, cos_t, sin_t, qg, kg, bq, bk, seq)
    kx, vx = _ctx_kv(ctx.reshape(bsz * ctx_len, d), mods[0], g_mix0, w_in_bf[:, KV_COL:], b_in_p[:, KV_COL:],
                     kg, bk, ctx_row)
    conv_w = jnp.zeros((32, CONV_CH), F32).at[:CONV_WIDTH].set(ev_conv_w[0])
    conv_h = _conv(u.reshape(bsz, seq, CONV_CH), conv_w, row2(ev_conv_b[0]), row2(ev_conv_ln_g[0]),
                   row2(ev_conv_ln_b[0]))
    per_kv = N_Q_HEADS // N_KV_HEADS
    sink_perm = ev_sink[0]
    r3 = lambda a, w: a.reshape(bsz, -1, w)
    att = _attention(sink_perm, r3(qa, Q_DIM), r3(qb, Q_DIM), r3(k, KV_DIM), r3(v, KV_DIM),
                     r3(kx, KV_DIM), r3(vx, KV_DIM))
    w_out = ev_w_out[0]
    w_out_conv = w_out[:CONV_CH].astype(BF16)
    w_out_att = w_out[CONV_CH:][perm].astype(BF16)

    def router(l):
        return moe_router_w[l].T.astype(BF16), moe_router_b[l].reshape(N_EXPERTS, 1)

    rwt, rb = router(0)
    h1, hf, idx3, gates3 = _post([conv_h.reshape(bsz * seq, CONV_CH), att.reshape(bsz * seq, Q_DIM)],
                                 [w_out_conv, w_out_att], row2(ev_b_out[0]), x2, mods[0],
                                 row2(norm_ffn_g[0]), rwt, rb, seq)
    h = _moe(h1, hf, idx3, gates3, mods[0], moe_w1[0], moe_b1[0], moe_w2[0], moe_b2[0], seq)

    a = _gmlp(h, mods[1], row2(norm_mix_g[1]), od_w_in[0].astype(BF16), row2(od_b_in[0]), row2(od_v_ln_g[0]),
              row2(od_v_ln_b[0]), od_w_s[0].astype(BF16), od_b_s[0].T, seq)
    rwt, rb = router(1)
    h1, hf, idx3, gates3 = _post([a], [od_w_out[0].astype(BF16)], row2(od_b_out[0]), h, mods[1],
                                 row2(norm_ffn_g[1]), rwt, rb, seq)
    h = _moe(h1, hf, idx3, gates3, mods[1], moe_w1[1], moe_b1[1], moe_w2[1], moe_b2[1], seq)
    return h.reshape(bsz, seq, d)
```

```python
import functools
import math

import jax
import jax.numpy as jnp
from jax import lax
from jax.experimental import pallas as pl
from jax.experimental.pallas import tpu as pltpu

F32 = jnp.float32
BF16 = jnp.bfloat16

D_MODEL = 1024
HEAD_DIM = 64
N_Q_HEADS = 8
N_KV_HEADS = 2
Q_DIM = N_Q_HEADS * HEAD_DIM
KV_DIM = N_KV_HEADS * HEAD_DIM
CONV_CH = 512
CONV_WIDTH = 31
CONV_HALO = 16
KV_COL = 2 * CONV_CH + Q_DIM
EVEN_IN = KV_COL + 2 * KV_DIM
ATT_BLOCK = 128
GRID_W = 64
ROPE_THETA = 10000.0
CHUNK = 128
GMLP_GROUPS = 8
N_EXPERTS = 32
TOP_K = 4
EXPERT_FF = 1024
SWIGLU_LIMIT = 7.0
SWIGLU_ALPHA = 1.702
EPS = 1e-6
NEG_INF = -1e30

TM = 512
EXPERT_BM = 256
MOD_ROWS = 16
VMEM_LIMIT = 48 * 1024 * 1024
EXPERT_VMEM_LIMIT = 56 * 1024 * 1024


def _cparams(sem):
    return pltpu.CompilerParams(dimension_semantics=sem, vmem_limit_bytes=VMEM_LIMIT)


def _sigmoid(x):
    return 1.0 / (1.0 + jnp.exp(-x))


def _bdot(a, b):
    return jnp.dot(a.astype(BF16), b.astype(BF16), preferred_element_type=F32)


def _ada_kernel(cc_ref, w_ref, b_ref, o_ref):
    a = cc_ref[...]
    o_ref[0] = _bdot(a * _sigmoid(a), w_ref[0]) + b_ref[0]


def _ada(cc, ada_w, ada_b):
    depth, d, n = ada_w.shape
    tn = 1536
    return pl.pallas_call(
        _ada_kernel,
        grid=(depth, n // tn),
        in_specs=[
            pl.BlockSpec((MOD_ROWS, d), lambda l, j: (0, 0)),
            pl.BlockSpec((1, d, tn), lambda l, j: (l, 0, j)),
            pl.BlockSpec((1, 1, tn), lambda l, j: (l, 0, j)),
        ],
        out_specs=pl.BlockSpec((1, MOD_ROWS, tn), lambda l, j: (l, 0, j)),
        out_shape=jax.ShapeDtypeStruct((depth, MOD_ROWS, n), F32),
        compiler_params=_cparams(("arbitrary", "arbitrary")),
        name="ada",
    )(cc, ada_w, ada_b.reshape(depth, 1, n))


def _mod_row(mod_ref, row, part):
    return mod_ref[pl.ds(row, 1), pl.ds(part * D_MODEL, D_MODEL)]


def _modulate(x, g, shift, scale):
    ms = jnp.mean(x * x, axis=-1, keepdims=True)
    return x * lax.rsqrt(ms + EPS) * g * (1.0 + scale) + shift


SLABS = D_MODEL // 128


def _store_slabs(ref, value):
    n = value.shape[0]
    for s in range(SLABS):
        ref[pl.ds(s, n, stride=SLABS), :] = value[:, 128 * s:128 * (s + 1)]


def _load_slabs(ref, n):
    return jnp.concatenate([ref[pl.ds(s, n, stride=SLABS), :] for s in range(SLABS)], axis=1)


def _head_rms(t, blockdiag, g):
    t2 = t * t
    hi = t2.astype(BF16)
    lo = (t2 - hi.astype(F32)).astype(BF16)
    ms = (jnp.dot(hi, blockdiag, preferred_element_type=F32)
          + jnp.dot(lo, blockdiag, preferred_element_type=F32))
    return t * lax.rsqrt(ms + EPS) * g


def _rope(t, cos, sin_signed):
    width = t.shape[-1]
    lane = lax.broadcasted_iota(jnp.int32, t.shape, 1)
    first = (lane % HEAD_DIM) < (HEAD_DIM // 2)
    partner = jnp.where(first, pltpu.roll(t, width - HEAD_DIM // 2, 1), pltpu.roll(t, HEAD_DIM // 2, 1))
    return t * cos + partner * sin_signed


def _inproj_kernel(x_ref, mod_ref, g_ref, w_ref, b_ref, cos_ref, sin_ref, qg_ref, kg_ref, bq_ref, bk_ref,
                   u_ref, qa_ref, qb_ref, k_ref, v_ref, *, tiles_per_batch):
    row = pl.program_id(0) // tiles_per_batch
    hm = _modulate(x_ref[...], g_ref[...], _mod_row(mod_ref, row, 0), _mod_row(mod_ref, row, 1))
    y = _bdot(hm, w_ref[...]) + b_ref[...]
    u_ref[...] = y[:, :CONV_CH] * _sigmoid(y[:, CONV_CH:2 * CONV_CH])
    cos = cos_ref[...]
    sin = sin_ref[...]
    q = _head_rms(y[:, 2 * CONV_CH:KV_COL], bq_ref[...], qg_ref[...])
    q = _rope(q, cos, sin) * (HEAD_DIM ** -0.5)
    lane = lax.broadcasted_iota(jnp.int32, q.shape, 1)
    low = (lane % (2 * HEAD_DIM)) < HEAD_DIM
    qa_ref[...] = jnp.where(low, q, 0.0).astype(BF16)
    qb_ref[...] = jnp.where(low, 0.0, q).astype(BF16)
    k = _head_rms(y[:, KV_COL:KV_COL + KV_DIM], bk_ref[...], kg_ref[...])
    k_ref[...] = _rope(k, cos[:, :KV_DIM], sin[:, :KV_DIM]).astype(BF16)
    v_ref[...] = y[:, KV_COL + KV_DIM:].astype(BF16)


def _inproj(x2, mod, g, w_bf, b, cos_t, sin_t, qg, kg, bq, bk, seq):
    t, d = x2.shape
    tpb = seq // TM
    const = lambda i: (0, 0)
    tile = lambda i: (i, 0)
    pos = lambda i: (i % tpb, 0)
    return pl.pallas_call(
        functools.partial(_inproj_kernel, tiles_per_batch=tpb),
        grid=(t // TM,),
        in_specs=[
            pl.BlockSpec((TM, d), tile),
            pl.BlockSpec(mod.shape, const),
            pl.BlockSpec((1, d), const),
            pl.BlockSpec(w_bf.shape, const),
            pl.BlockSpec((1, EVEN_IN), const),
            pl.BlockSpec((TM, Q_DIM), pos),
            pl.BlockSpec((TM, Q_DIM), pos),
            pl.BlockSpec((1, Q_DIM), const),
            pl.BlockSpec((1, KV_DIM), const),
            pl.BlockSpec((Q_DIM, Q_DIM), const),
            pl.BlockSpec((KV_DIM, KV_DIM), const),
        ],
        out_specs=[
            pl.BlockSpec((TM, CONV_CH), tile),
            pl.BlockSpec((TM, Q_DIM), tile),
            pl.BlockSpec((TM, Q_DIM), tile),
            pl.BlockSpec((TM, KV_DIM), tile),
            pl.BlockSpec((TM, KV_DIM), tile),
        ],
        out_shape=[
            jax.ShapeDtypeStruct((t, CONV_CH), F32),
            jax.ShapeDtypeStruct((t, Q_DIM), BF16),
            jax.ShapeDtypeStruct((t, Q_DIM), BF16),
            jax.ShapeDtypeStruct((t, KV_DIM), BF16),
            jax.ShapeDtypeStruct((t, KV_DIM), BF16),
        ],
        compiler_params=_cparams(("arbitrary",)),
        name="inproj",
    )(x2, mod, g, w_bf, b, cos_t, sin_t, qg, kg, bq, bk)


def _ctx_kv_kernel(g_in_ref, mod_ref, g_ref, w_ref, b_ref, kg_ref, bk_ref, k_ref, v_ref, *, mod_row):
    gm = _modulate(g_in_ref[...], g_ref[...], _mod_row(mod_ref, mod_row, 0), _mod_row(mod_ref, mod_row, 1))
    y = _bdot(gm, w_ref[...]) + b_ref[...]
    k_ref[...] = _head_rms(y[:, :KV_DIM], bk_ref[...], kg_ref[...]).astype(BF16)
    v_ref[...] = y[:, KV_DIM:].astype(BF16)


def _ctx_kv(ctx2, mod, g, w_kv_bf, b_kv, kg, bk, mod_row):
    t, d = ctx2.shape
    const = lambda i: (0, 0)
    tile = lambda i: (i, 0)
    return pl.pallas_call(
        functools.partial(_ctx_kv_kernel, mod_row=mod_row),
        grid=(t // TM,),
        in_specs=[
            pl.BlockSpec((TM, d), tile),
            pl.BlockSpec(mod.shape, const),
            pl.BlockSpec((1, d), const),
            pl.BlockSpec(w_kv_bf.shape, const),
            pl.BlockSpec((1, 2 * KV_DIM), const),
            pl.BlockSpec((1, KV_DIM), const),
            pl.BlockSpec((KV_DIM, KV_DIM), const),
        ],
        out_specs=[pl.BlockSpec((TM, KV_DIM), tile), pl.BlockSpec((TM, KV_DIM), tile)],
        out_shape=[jax.ShapeDtypeStruct((t, KV_DIM), BF16), jax.ShapeDtypeStruct((t, KV_DIM), BF16)],
        compiler_params=_cparams(("arbitrary",)),
        name="ctx_kv",
    )(ctx2, mod, g, w_kv_bf, b_kv, kg, bk)


CONV_ROWS = 64


def _conv_kernel(u_ref, w_ref, cb_ref, lg_ref, lb_ref, o_ref, pad_ref):
    seq = u_ref.shape[1]
    zeros = jnp.zeros((CONV_HALO, CONV_CH), F32)
    pad_ref[pl.ds(0, CONV_HALO), :] = zeros
    pad_ref[pl.ds(CONV_HALO, seq), :] = u_ref[0]
    pad_ref[pl.ds(CONV_HALO + seq, CONV_HALO), :] = zeros
    first = CONV_HALO - CONV_WIDTH // 2

    def body(i, carry):
        t0 = pl.multiple_of(i * CONV_ROWS, CONV_ROWS)
        win = pad_ref[pl.ds(t0, CONV_ROWS + 2 * CONV_HALO), :]
        acc = jnp.zeros((CONV_ROWS, CONV_CH), F32) + cb_ref[...]
        for tap in range(CONV_WIDTH):
            acc = acc + w_ref[tap:tap + 1, :] * win[first + tap:first + tap + CONV_ROWS, :]
        mu = jnp.mean(acc, axis=-1, keepdims=True)
        cen = acc - mu
        var = jnp.mean(cen * cen, axis=-1, keepdims=True)
        yn = cen * lax.rsqrt(var + EPS) * lg_ref[...] + lb_ref[...]
        o_ref[0, pl.ds(t0, CONV_ROWS), :] = (yn * _sigmoid(yn)).astype(BF16)
        return carry

    lax.fori_loop(0, seq // CONV_ROWS, body, 0)


def _conv(u3, w_pad, cb, lg, lb):
    bsz, seq, ch = u3.shape
    const = lambda b: (0, 0)
    return pl.pallas_call(
        _conv_kernel,
        grid=(bsz,),
        in_specs=[
            pl.BlockSpec((1, seq, ch), lambda b: (b, 0, 0)),
            pl.BlockSpec(w_pad.shape, const),
            pl.BlockSpec((1, ch), const),
            pl.BlockSpec((1, ch), const),
            pl.BlockSpec((1, ch), const),
        ],
        out_specs=pl.BlockSpec((1, seq, ch), lambda b: (b, 0, 0)),
        out_shape=jax.ShapeDtypeStruct((bsz, seq, ch), BF16),
        scratch_shapes=[pltpu.VMEM((seq + 2 * CONV_HALO, ch), F32)],
        compiler_params=_cparams(("arbitrary",)),
        name="conv",
    )(u3, w_pad, cb, lg, lb)


def _attn_kernel(sink_ref, qa_ref, qb_ref, kp_ref, kc_ref, kn_ref, kx_ref, vp_ref, vc_ref, vn_ref, vx_ref, o_ref):
    n = pl.program_id(1)
    nb = pl.num_programs(1)
    kcat = jnp.concatenate([kp_ref[0], kc_ref[0], kn_ref[0], kx_ref[0]], axis=0)
    vcat = jnp.concatenate([vp_ref[0], vc_ref[0], vn_ref[0], vx_ref[0]], axis=0)
    blk = ATT_BLOCK
    qi = lax.broadcasted_iota(jnp.int32, (blk, blk), 0)
    kj = lax.broadcasted_iota(jnp.int32, (blk, blk), 1)
    mask_prev = (kj >= qi) & (n > 0)
    mask_next = (kj <= qi) & (n < nb - 1)
    n_chunks = Q_DIM // (2 * HEAD_DIM)
    res = []
    for kvh, q_ref in enumerate((qa_ref, qb_ref)):
        qv = q_ref[0]
        q4 = jnp.concatenate([qv[:, 128 * j:128 * (j + 1)] for j in range(n_chunks)], axis=0)
        s = lax.dot_general(q4, kcat, (((1,), (1,)), ((), ())), preferred_element_type=F32)
        outs = []
        for j in range(n_chunks):
            sj = s[blk * j:blk * (j + 1)]
            sp = jnp.where(mask_prev, sj[:, :blk], NEG_INF)
            sc = sj[:, blk:2 * blk]
            sn = jnp.where(mask_next, sj[:, 2 * blk:3 * blk], NEG_INF)
            sx = sj[:, 3 * blk:]
            sink = sink_ref[kvh * n_chunks + j]
            m = jnp.maximum(jnp.maximum(jnp.max(sp, axis=-1, keepdims=True), jnp.max(sc, axis=-1, keepdims=True)),
                            jnp.maximum(jnp.max(sn, axis=-1, keepdims=True), jnp.max(sx, axis=-1, keepdims=True)))
            m = jnp.maximum(m, sink)
            parts = [jnp.exp(t - m) for t in (sp, sc, sn, sx)]
            den = sum(jnp.sum(p, axis=-1, keepdims=True) for p in parts) + jnp.exp(sink - m)
            pcat = jnp.concatenate(parts, axis=-1).astype(BF16)
            o = jnp.dot(pcat, vcat, preferred_element_type=F32)
            outs.append(o * (1.0 / den))
        res.append(outs)
    lane = lax.broadcasted_iota(jnp.int32, (blk, 2 * HEAD_DIM), 1)
    low = lane < HEAD_DIM
    o_ref[0] = jnp.concatenate([jnp.where(low, res[0][j], res[1][j]) for j in range(n_chunks)],
                               axis=-1).astype(BF16)


def _attention(sink, qa, qb, k, v, kx, vx):
    bsz, seq, _ = qa.shape
    nb = seq // ATT_BLOCK
    ctx_len = kx.shape[1]
    cur = lambda b, n, s: (b, n, 0)
    prev = lambda b, n, s: (b, jnp.maximum(n - 1, 0), 0)
    nxt = lambda b, n, s: (b, jnp.minimum(n + 1, nb - 1), 0)
    whole = lambda b, n, s: (b, 0, 0)
    kvspec = lambda im: pl.BlockSpec((1, ATT_BLOCK, KV_DIM), im)
    return pl.pallas_call(
        _attn_kernel,
        grid_spec=pltpu.PrefetchScalarGridSpec(
            num_scalar_prefetch=1,
            grid=(bsz, nb),
            in_specs=[
                pl.BlockSpec((1, ATT_BLOCK, Q_DIM), cur),
                pl.BlockSpec((1, ATT_BLOCK, Q_DIM), cur),
                kvspec(prev), kvspec(cur), kvspec(nxt),
                pl.BlockSpec((1, ctx_len, KV_DIM), whole),
                kvspec(prev), kvspec(cur), kvspec(nxt),
                pl.BlockSpec((1, ctx_len, KV_DIM), whole),
            ],
            out_specs=pl.BlockSpec((1, ATT_BLOCK, Q_DIM), cur),
        ),
        out_shape=jax.ShapeDtypeStruct((bsz, seq, Q_DIM), BF16),
        compiler_params=_cparams(("arbitrary", "arbitrary")),
        name="attention",
    )(sink, qa, qb, k, k, k, kx, v, v, v, vx)


def _erf(x):
    return lax.erf(x)


def _gmlp_kernel(x_ref, mod_ref, g_ref, w_ref, b_ref, lg_ref, lb_ref, ws_ref, bst_ref, a_ref, *, tiles_per_batch):
    row = pl.program_id(0) // tiles_per_batch
    hm = _modulate(x_ref[...], g_ref[...], _mod_row(mod_ref, row, 0), _mod_row(mod_ref, row, 1))
    z = _bdot(hm, w_ref[...]) + b_ref[...]
    z = 0.5 * z * (1.0 + _erf(z * (1.0 / math.sqrt(2.0))))
    half = z.shape[1] // 2
    u = z[:, :half]
    v = z[:, half:]
    mu = jnp.mean(v, axis=-1, keepdims=True)
    cen = v - mu
    var = jnp.mean(cen * cen, axis=-1, keepdims=True)
    vn = (cen * lax.rsqrt(var + EPS) * lg_ref[...] + lb_ref[...]).astype(BF16)
    gch = half // GMLP_GROUPS
    for c in range(x_ref.shape[0] // CHUNK):
        rows = slice(c * CHUNK, (c + 1) * CHUNK)
        for g in range(GMLP_GROUPS):
            cols = slice(g * gch, (g + 1) * gch)
            sv = jnp.dot(ws_ref[g], vn[rows, cols], preferred_element_type=F32) + bst_ref[:, g:g + 1]
            a_ref[rows, cols] = (u[rows, cols] * sv).astype(BF16)


def _gmlp(x2, mod, g, w_bf, b, lg, lb, ws_bf, bst, seq):
    t, d = x2.shape
    tpb = seq // TM
    const = lambda i: (0, 0)
    tile = lambda i: (i, 0)
    n_in = w_bf.shape[1]
    return pl.pallas_call(
        functools.partial(_gmlp_kernel, tiles_per_batch=tpb),
        grid=(t // TM,),
        in_specs=[
            pl.BlockSpec((TM, d), tile),
            pl.BlockSpec(mod.shape, const),
            pl.BlockSpec((1, d), const),
            pl.BlockSpec(w_bf.shape, const),
            pl.BlockSpec((1, n_in), const),
            pl.BlockSpec((1, n_in // 2), const),
            pl.BlockSpec((1, n_in // 2), const),
            pl.BlockSpec(ws_bf.shape, lambda i: (0, 0, 0)),
            pl.BlockSpec(bst.shape, const),
        ],
        out_specs=pl.BlockSpec((TM, n_in // 2), tile),
        out_shape=jax.ShapeDtypeStruct((t, n_in // 2), BF16),
        compiler_params=_cparams(("arbitrary",)),
        name="gmlp",
    )(x2, mod, g, w_bf, b, lg, lb, ws_bf, bst)


def _post_kernel(*refs, n_parts, tiles_per_batch):
    a_refs = refs[:n_parts]
    w_refs = refs[n_parts:2 * n_parts]
    bo_ref, x_ref, mod_ref, gf_ref, rwt_ref, rb_ref, h1_ref, hf_ref, idx_ref, gate_ref = refs[2 * n_parts:]
    row = pl.program_id(0) // tiles_per_batch
    mix = bo_ref[...]
    for a_ref, w_ref in zip(a_refs, w_refs):
        mix = mix + jnp.dot(a_ref[...], w_ref[...], preferred_element_type=F32)
    h1 = x_ref[...] + _mod_row(mod_ref, row, 2) * mix
    h1_ref[...] = h1
    hf = _modulate(h1, gf_ref[...], _mod_row(mod_ref, row, 3), _mod_row(mod_ref, row, 4))
    _store_slabs(hf_ref, hf)
    logits = lax.dot_general(rwt_ref[...], hf.astype(BF16), (((1,), (1,)), ((), ())),
                             preferred_element_type=F32) + rb_ref[...]
    eid = lax.broadcasted_iota(jnp.int32, logits.shape, 0)
    vals, idxs = [], []
    for _ in range(TOP_K):
        m = jnp.max(logits, axis=0, keepdims=True)
        pick = jnp.min(jnp.where(logits == m, eid, N_EXPERTS), axis=0, keepdims=True)
        vals.append(m)
        idxs.append(pick)
        logits = jnp.where(eid == pick, -jnp.inf, logits)
    exps = [jnp.exp(v - vals[0]) for v in vals]
    inv = 1.0 / sum(exps)
    tm = logits.shape[1]
    idx_ref[0] = jnp.concatenate(idxs + [jnp.zeros((8 - TOP_K, tm), jnp.int32)], axis=0)
    gate_ref[0] = jnp.concatenate([e * inv for e in exps] + [jnp.zeros((8 - TOP_K, tm), F32)], axis=0)


def _post(a_parts, w_parts, b_out, x2, mod, gf, rwt_bf, rb_col, seq):
    t, d = x2.shape
    tpb = seq // TM
    nt = t // TM
    const = lambda i: (0, 0)
    tile = lambda i: (i, 0)
    in_specs = ([pl.BlockSpec((TM, a.shape[1]), tile) for a in a_parts]
                + [pl.BlockSpec(w.shape, const) for w in w_parts]
                + [pl.BlockSpec((1, d), const), pl.BlockSpec((TM, d), tile), pl.BlockSpec(mod.shape, const),
                   pl.BlockSpec((1, d), const), pl.BlockSpec(rwt_bf.shape, const),
                   pl.BlockSpec(rb_col.shape, const)])
    return pl.pallas_call(
        functools.partial(_post_kernel, n_parts=len(a_parts), tiles_per_batch=tpb),
        grid=(nt,),
        in_specs=in_specs,
        out_specs=[
            pl.BlockSpec((TM, d), tile),
            pl.BlockSpec((TM * SLABS, 128), tile),
            pl.BlockSpec((1, 8, TM), lambda i: (i, 0, 0)),
            pl.BlockSpec((1, 8, TM), lambda i: (i, 0, 0)),
        ],
        out_shape=[
            jax.ShapeDtypeStruct((t, d), F32),
            jax.ShapeDtypeStruct((t * SLABS, 128), F32),
            jax.ShapeDtypeStruct((nt, 8, TM), jnp.int32),
            jax.ShapeDtypeStruct((nt, 8, TM), F32),
        ],
        compiler_params=_cparams(("arbitrary",)),
        name="post",
    )(*a_parts, *w_parts, b_out, x2, mod, gf, rwt_bf, rb_col)


META_LANES = 128


def _max_blocks(n_tokens):
    return (n_tokens * TOP_K + N_EXPERTS * (EXPERT_BM - 1) + EXPERT_BM - 1) // EXPERT_BM


def _route_kernel(idx_ref, dest_ref, meta_ref, pfull_ref):
    nt, _, tc = idx_ref.shape
    e = N_EXPERTS
    eid = lax.broadcasted_iota(jnp.int32, (e, tc), 0)
    r = lax.broadcasted_iota(jnp.int32, (tc, tc), 0)
    c = lax.broadcasted_iota(jnp.int32, (tc, tc), 1)
    upper = (r < c).astype(BF16)

    def onehots(ci):
        blk = idx_ref[ci]
        return [eid == blk[k:k + 1, :] for k in range(TOP_K)]

    def count_body(ci, carry):
        oh = onehots(ci)
        member = sum(o.astype(F32) for o in oh)
        pfull_ref[ci] = jnp.dot(member.astype(BF16), upper, preferred_element_type=F32) + carry
        return carry + jnp.sum(member, axis=1, keepdims=True)

    counts = lax.fori_loop(0, nt, count_body, jnp.zeros((e, 1), F32))
    nblk = jnp.floor((counts + (EXPERT_BM - 1)) * (1.0 / EXPERT_BM))
    rr = lax.broadcasted_iota(jnp.int32, (e, e), 0)
    cc = lax.broadcasted_iota(jnp.int32, (e, e), 1)
    lower = (cc <= rr).astype(BF16)
    pend_b = jnp.dot(lower, jnp.broadcast_to(nblk, (e, 128)).astype(BF16),
                     preferred_element_type=F32)[:, 0:1]
    pstart = (pend_b - nblk) * EXPERT_BM

    def dest_body(ci, carry):
        oh = onehots(ci)
        base = pfull_ref[ci] + pstart
        rows = [jnp.sum(jnp.where(o, base, 0.0), axis=0, keepdims=True) for o in oh]
        dest_ref[ci] = jnp.concatenate(rows + [jnp.zeros((8 - TOP_K, tc), F32)], axis=0).astype(jnp.int32)
        return carry

    lax.fori_loop(0, nt, dest_body, 0)
    lanes = meta_ref.shape[1]
    bid = lax.broadcasted_iota(jnp.int32, (e, lanes), 1)
    sub = lax.broadcasted_iota(jnp.int32, (e, lanes), 0)
    n_used = jnp.broadcast_to(pend_b[e - 1:e, :], (1, lanes))
    to_lanes = lambda col: jnp.sum(jnp.where(sub == bid, col, 0.0), axis=0, keepdims=True)
    meta_ref[...] = jnp.concatenate([n_used, to_lanes(nblk), to_lanes(pend_b - nblk),
                                     jnp.zeros((5, lanes), F32)], axis=0).astype(jnp.int32)


def _route(idx3):
    nt, _, tc = idx3.shape
    return pl.pallas_call(
        _route_kernel,
        out_shape=[jax.ShapeDtypeStruct((nt, 8, tc), jnp.int32),
                   jax.ShapeDtypeStruct((8, META_LANES), jnp.int32)],
        scratch_shapes=[pltpu.VMEM((nt, N_EXPERTS, tc), F32)],
        compiler_params=pltpu.CompilerParams(vmem_limit_bytes=VMEM_LIMIT),
        name="route",
    )(idx3)


def _invert_kernel(dest_ref, src_ref, zeros_ref, map_ref, sem):
    tm = dest_ref.shape[2]
    step = pl.program_id(0)

    @pl.when(step == 0)
    def _():
        zeros_ref[...] = jnp.zeros(zeros_ref.shape, zeros_ref.dtype)
        cp = pltpu.make_async_copy(zeros_ref, map_ref, sem)
        cp.start()
        cp.wait()

    def body(j, carry):
        for k in range(TOP_K):
            map_ref[dest_ref[0, k, j]] = (step * tm + j) * TOP_K + k
        return carry

    lax.fori_loop(0, tm, body, 0, unroll=8)

    @pl.when(step == pl.num_programs(0) - 1)
    def _():
        cp = pltpu.make_async_copy(map_ref, src_ref, sem)
        cp.start()
        cp.wait()


def _invert(dest3, n_rows):
    nt = dest3.shape[0]
    return pl.pallas_call(
        _invert_kernel,
        grid=(nt,),
        in_specs=[pl.BlockSpec((1, 8, TM), lambda i: (i, 0, 0), memory_space=pltpu.SMEM)],
        out_specs=pl.BlockSpec(memory_space=pl.ANY),
        out_shape=jax.ShapeDtypeStruct((n_rows,), jnp.int32),
        scratch_shapes=[pltpu.VMEM((n_rows,), jnp.int32), pltpu.SMEM((n_rows,), jnp.int32),
                        pltpu.SemaphoreType.DMA],
        compiler_params=_cparams(("arbitrary",)),
        name="invert",
    )(dest3)


BLOCK_SLAB_ROWS = EXPERT_BM * SLABS


def _expert_kernel(nu_ref, nblk_ref, bstart_ref, src_ref, hf_ref, w1_ref, b1_ref, w2_ref, b2_ref, y_ref,
                   w1b_ref, w2b_ref, xbuf_ref, ybuf_ref, gsem, ysem):
    ex = pl.program_id(0)
    n_used = nu_ref[0]
    n_mine = nblk_ref[ex]
    first = bstart_ref[ex]

    def row_gather(block, slot, j):
        tok = lax.shift_right_logical(src_ref[block * EXPERT_BM + j], 2)
        return pltpu.make_async_copy(hf_ref.at[pl.ds(pl.multiple_of(tok * SLABS, SLABS), SLABS)],
                                     xbuf_ref.at[slot, pl.ds(j * SLABS, SLABS)], gsem.at[slot])

    def y_copy(block, slot):
        rows = pl.ds(pl.multiple_of(block * BLOCK_SLAB_ROWS, BLOCK_SLAB_ROWS), BLOCK_SLAB_ROWS)
        return pltpu.make_async_copy(ybuf_ref.at[slot], y_ref.at[rows], ysem.at[slot])

    @pl.when(ex == 0)
    def _():
        for j in range(EXPERT_BM):
            row_gather(0, 0, j).start()

    @pl.when(n_mine > 0)
    def _():
        w1b_ref[...] = w1_ref[0, 0].astype(BF16)
        w2b_ref[...] = w2_ref[0, 0].astype(BF16)

    def block_body(i, carry):
        b = first + i
        slot = lax.rem(b, 2)

        @pl.when(b >= 2)
        def _():
            y_copy(b - 2, slot).wait()

        ahead = jnp.minimum(b + 1, n_used - 1)
        for j in range(EXPERT_BM):
            row_gather(b, slot, j).wait()
        x = _load_slabs(xbuf_ref.at[slot], EXPERT_BM).astype(BF16)
        for j in range(EXPERT_BM):
            row_gather(ahead, 1 - slot, j).start()
        h = jnp.dot(x, w1b_ref[...], preferred_element_type=F32) + b1_ref[0, 0]
        gate = jnp.minimum(h[:, :EXPERT_FF], SWIGLU_LIMIT)
        up = jnp.clip(h[:, EXPERT_FF:], -SWIGLU_LIMIT, SWIGLU_LIMIT)
        act = (up + 1.0) * gate * _sigmoid(gate * SWIGLU_ALPHA)
        _store_slabs(ybuf_ref.at[slot],
                     jnp.dot(act.astype(BF16), w2b_ref[...], preferred_element_type=F32) + b2_ref[0, 0])
        y_copy(b, slot).start()
        return carry

    lax.fori_loop(0, n_mine, block_body, 0)

    @pl.when(ex == pl.num_programs(0) - 1)
    def _():
        last = n_used - 1
        spare = lax.rem(n_used, 2)
        for j in range(EXPERT_BM):
            row_gather(last, spare, j).wait()
        y_copy(last, lax.rem(last, 2)).wait()

        @pl.when(n_used >= 2)
        def _():
            y_copy(last - 1, spare).wait()

        ybuf_ref[0] = jnp.zeros(ybuf_ref.shape[1:], ybuf_ref.dtype)
        n_total = y_ref.shape[0] // BLOCK_SLAB_ROWS

        def tail_start(b, carry):
            y_copy(b, 0).start()
            return carry

        def tail_wait(b, carry):
            y_copy(b, 0).wait()
            return carry

        lax.fori_loop(n_used, n_total, tail_start, 0)
        lax.fori_loop(n_used, n_total, tail_wait, 0)


def _experts(layer, n_used, nblk, bstart, src, hf2, w1, b1, w2, b2):
    n_rows = src.shape[0]
    _, n_exp, d, ff2 = w1.shape
    sel = lambda e, *_: (layer, e, 0, 0)
    return pl.pallas_call(
        _expert_kernel,
        grid_spec=pltpu.PrefetchScalarGridSpec(
            num_scalar_prefetch=4,
            grid=(n_exp,),
            in_specs=[
                pl.BlockSpec(memory_space=pl.ANY),
                pl.BlockSpec((1, 1, d, ff2), sel),
                pl.BlockSpec((1, 1, 1, ff2), sel),
                pl.BlockSpec((1, 1, ff2 // 2, d), sel),
                pl.BlockSpec((1, 1, 1, d), sel),
            ],
            out_specs=pl.BlockSpec(memory_space=pl.ANY),
            scratch_shapes=[
                pltpu.VMEM((d, ff2), BF16),
                pltpu.VMEM((ff2 // 2, d), BF16),
                pltpu.VMEM((2, BLOCK_SLAB_ROWS, 128), F32),
                pltpu.VMEM((2, BLOCK_SLAB_ROWS, 128), F32),
                pltpu.SemaphoreType.DMA((2,)),
                pltpu.SemaphoreType.DMA((2,)),
            ],
        ),
        out_shape=jax.ShapeDtypeStruct((n_rows * SLABS, 128), F32),
        compiler_params=pltpu.CompilerParams(dimension_semantics=("arbitrary",),
                                             vmem_limit_bytes=EXPERT_VMEM_LIMIT),
        name="experts",
    )(n_used, nblk, bstart, src, hf2, w1, b1.reshape(b1.shape[0], n_exp, 1, ff2), w2,
      b2.reshape(b2.shape[0], n_exp, 1, d))


def _combine_kernel(dest_ref, gate_ref, h1_ref, mod_ref, y_ref, o_ref, ybuf_ref, sem, *, tiles_per_batch):
    tm = h1_ref.shape[0]
    row = pl.program_id(0) // tiles_per_batch

    def row_copy(j, k):
        return pltpu.make_async_copy(y_ref.at[dest_ref[0, k, j]],
                                     ybuf_ref.at[k, pl.ds(pl.multiple_of(j * SLABS, SLABS), SLABS)], sem)

    def start_body(j, carry):
        for k in range(TOP_K):
            row_copy(j, k).start(priority=k % 2)
        return carry

    def wait_body(j, carry):
        for k in range(TOP_K):
            row_copy(j, k).wait()
        return carry

    lax.fori_loop(0, tm, start_body, 0, unroll=8)
    lax.fori_loop(0, tm, wait_body, 0, unroll=8)
    gates = gate_ref[0].T
    acc = gates[:, 0:1] * _load_slabs(ybuf_ref.at[0], tm)
    for k in range(1, TOP_K):
        acc = acc + gates[:, k:k + 1] * _load_slabs(ybuf_ref.at[k], tm)
    o_ref[...] = h1_ref[...] + _mod_row(mod_ref, row, 5) * acc


def _combine(dest3, gates3, h1, mod, y_rows, seq):
    t, d = h1.shape
    tpb = seq // TM
    return pl.pallas_call(
        functools.partial(_combine_kernel, tiles_per_batch=tpb),
        grid=(t // TM,),
        in_specs=[
            pl.BlockSpec((1, 8, TM), lambda i: (i, 0, 0), memory_space=pltpu.SMEM),
            pl.BlockSpec((1, 8, TM), lambda i: (i, 0, 0)),
            pl.BlockSpec((TM, d), lambda i: (i, 0)),
            pl.BlockSpec(mod.shape, lambda i: (0, 0)),
            pl.BlockSpec(memory_space=pl.ANY),
        ],
        out_specs=pl.BlockSpec((TM, d), lambda i: (i, 0)),
        out_shape=jax.ShapeDtypeStruct((t, d), F32),
        scratch_shapes=[pltpu.VMEM((TOP_K, TM * SLABS, 128), F32), pltpu.SemaphoreType.DMA],
        compiler_params=_cparams(("arbitrary",)),
        name="combine",
    )(dest3, gates3, h1, mod, y_rows)


def _moe(layer, h1, hf2, idx3, gates3, mod, w1, b1, w2, b2, seq):
    dest3, meta = _route(idx3)
    src = _invert(dest3, _max_blocks(h1.shape[0]) * EXPERT_BM)
    y2 = _experts(layer, meta[0, :1], meta[1, :N_EXPERTS], meta[2, :N_EXPERTS], src, hf2, w1, b1, w2, b2)
    return _combine(dest3, gates3, h1, mod, y2.reshape(-1, SLABS, 128), seq)


def _rope_tables(seq):
    rows = seq // GRID_W
    r = jnp.repeat(jnp.arange(rows), GRID_W).astype(F32)
    col = jnp.tile(jnp.arange(GRID_W), rows).astype(F32)
    pairs = HEAD_DIM // 4
    inv_freq = ROPE_THETA ** (-jnp.arange(pairs, dtype=F32) / pairs)
    ang = jnp.concatenate([r[:, None] * inv_freq, col[:, None] * inv_freq], axis=-1)
    cos, sin = jnp.cos(ang), jnp.sin(ang)
    cos_t = jnp.tile(cos, (1, Q_DIM // (HEAD_DIM // 2)))
    sin_t = jnp.tile(jnp.concatenate([-sin, sin], axis=-1), (1, N_Q_HEADS))
    return cos_t, sin_t


def _block_diag_mean(width):
    i = jnp.arange(width)
    return jnp.where((i[:, None] // HEAD_DIM) == (i[None, :] // HEAD_DIM), 1.0 / HEAD_DIM, 0.0).astype(BF16)


def _head_perm():
    per_kv = N_Q_HEADS // N_KV_HEADS
    heads = []
    for j in range(per_kv):
        heads += [j, per_kv + j]
    return jnp.concatenate([jnp.arange(HEAD_DIM) + h * HEAD_DIM for h in heads])


def kernel(x, c, ctx, c_ctx, ada_w, ada_b, norm_mix_g, norm_ffn_g, ev_w_in, ev_b_in, ev_conv_w, ev_conv_b, ev_conv_ln_g, ev_conv_ln_b, ev_q_norm_g, ev_k_norm_g, ev_sink, ev_w_out, ev_b_out, od_w_in, od_b_in, od_v_ln_g, od_v_ln_b, od_w_s, od_b_s, od_w_out, od_b_out, moe_router_w, moe_router_b, moe_w1, moe_b1, moe_w2, moe_b2):
    bsz, seq, d = x.shape
    ctx_len = ctx.shape[1]
    assert d == D_MODEL and seq % TM == 0 and (bsz * ctx_len) % TM == 0 and bsz < MOD_ROWS
    assert ada_w.shape[0] == 2, "layer 0 is the conv/attention layer, layer 1 the gMLP layer"
    ctx_row = bsz
    cc = jnp.zeros((MOD_ROWS, d), F32).at[:bsz].set(c).at[ctx_row].set(c_ctx)
    mods = _ada(cc, ada_w, ada_b)
    row2 = lambda v: v.reshape(1, -1)

    perm = _head_perm()
    w_in = ev_w_in[0]
    w_in_bf = jnp.concatenate([w_in[:, :2 * CONV_CH], w_in[:, 2 * CONV_CH:KV_COL][:, perm], w_in[:, KV_COL:]],
                              axis=1).astype(BF16)
    b_in = ev_b_in[0]
    b_in_p = row2(jnp.concatenate([b_in[:2 * CONV_CH], b_in[2 * CONV_CH:KV_COL][perm], b_in[KV_COL:]]))
    cos_t, sin_t = _rope_tables(seq)
    qg = row2(jnp.tile(ev_q_norm_g[0], N_Q_HEADS))
    kg = row2(jnp.tile(ev_k_norm_g[0], N_KV_HEADS))
    bq, bk = _block_diag_mean(Q_DIM), _block_diag_mean(KV_DIM)
    x2 = x.reshape(bsz * seq, d)
    g_mix0 = row2(norm_mix_g[0])
    u, qa, qb, k, v = _inproj(x2, mods[0], g_mix0, w_in_bf, b_in_p, cos_t, sin_t, qg, kg, bq, bk, seq)
    kx, vx = _ctx_kv(ctx.reshape(bsz * ctx_len, d), mods[0], g_mix0, w_in_bf[:, KV_COL:], b_in_p[:, KV_COL:],
                     kg, bk, ctx_row)
    conv_w = jnp.zeros((32, CONV_CH), F32).at[:CONV_WIDTH].set(ev_conv_w[0])
    conv_h = _conv(u.reshape(bsz, seq, CONV_CH), conv_w, row2(ev_conv_b[0]), row2(ev_conv_ln_g[0]),
                   row2(ev_conv_ln_b[0]))
    r3 = lambda a, w: a.reshape(bsz, -1, w)
    att = _attention(ev_sink[0], r3(qa, Q_DIM), r3(qb, Q_DIM), r3(k, KV_DIM), r3(v, KV_DIM),
                     r3(kx, KV_DIM), r3(vx, KV_DIM))
    w_out = ev_w_out[0]
    w_out_conv = w_out[:CONV_CH].astype(BF16)
    w_out_att = w_out[CONV_CH:][perm].astype(BF16)

    def router(l):
        return moe_router_w[l].T.astype(BF16), moe_router_b[l].reshape(N_EXPERTS, 1)

    rwt, rb = router(0)
    h1, hf, idx3, gates3 = _post([conv_h.reshape(bsz * seq, CONV_CH), att.reshape(bsz * seq, Q_DIM)],
                                 [w_out_conv, w_out_att], row2(ev_b_out[0]), x2, mods[0],
                                 row2(norm_ffn_g[0]), rwt, rb, seq)
    h = _moe(0, h1, hf, idx3, gates3, mods[0], moe_w1, moe_b1, moe_w2, moe_b2, seq)

    a = _gmlp(h, mods[1], row2(norm_mix_g[1]), od_w_in[0].astype(BF16), row2(od_b_in[0]), row2(od_v_ln_g[0]),
              row2(od_v_ln_b[0]), od_w_s[0].astype(BF16), od_b_s[0].T, seq)
    rwt, rb = router(1)
    h1, hf, idx3, gates3 = _post([a], [od_w_out[0].astype(BF16)], row2(od_b_out[0]), h, mods[1],
                                 row2(norm_ffn_g[1]), rwt, rb, seq)
    h = _moe(1, h1, hf, idx3, gates3, mods[1], moe_w1, moe_b1, moe_w2, moe_b2, seq)
    return h.reshape(bsz, seq, d)
```

```python
import functools
import math

import jax
import jax.numpy as jnp
from jax import lax
from jax.experimental import pallas as pl
from jax.experimental.pallas import tpu as pltpu

F32 = jnp.float32
BF16 = jnp.bfloat16

D_MODEL = 1024
HEAD_DIM = 64
N_Q_HEADS = 8
N_KV_HEADS = 2
Q_DIM = N_Q_HEADS * HEAD_DIM
KV_DIM = N_KV_HEADS * HEAD_DIM
CONV_CH = 512
CONV_WIDTH = 31
CONV_HALO = 16
KV_COL = 2 * CONV_CH + Q_DIM
EVEN_IN = KV_COL + 2 * KV_DIM
ATT_BLOCK = 128
GRID_W = 64
ROPE_THETA = 10000.0
CHUNK = 128
GMLP_GROUPS = 8
N_EXPERTS = 32
TOP_K = 4
EXPERT_FF = 1024
SWIGLU_LIMIT = 7.0
SWIGLU_ALPHA = 1.702
EPS = 1e-6
NEG_INF = -1e30

TM = 512
EXPERT_BM = 256
MOD_ROWS = 16
VMEM_LIMIT = 48 * 1024 * 1024
EXPERT_VMEM_LIMIT = 56 * 1024 * 1024


def _cparams(sem):
    return pltpu.CompilerParams(dimension_semantics=sem, vmem_limit_bytes=VMEM_LIMIT)


def _sigmoid(x):
    return 1.0 / (1.0 + jnp.exp(-x))


def _bdot(a, b):
    return jnp.dot(a.astype(BF16), b.astype(BF16), preferred_element_type=F32)


def _ada_kernel(cc_ref, w_ref, b_ref, o_ref):
    a = cc_ref[...]
    o_ref[0] = _bdot(a * _sigmoid(a), w_ref[0]) + b_ref[0]


def _ada(cc, ada_w, ada_b):
    depth, d, n = ada_w.shape
    tn = 1536
    return pl.pallas_call(
        _ada_kernel,
        grid=(depth, n // tn),
        in_specs=[
            pl.BlockSpec((MOD_ROWS, d), lambda l, j: (0, 0)),
            pl.BlockSpec((1, d, tn), lambda l, j: (l, 0, j)),
            pl.BlockSpec((1, 1, tn), lambda l, j: (l, 0, j)),
        ],
        out_specs=pl.BlockSpec((1, MOD_ROWS, tn), lambda l, j: (l, 0, j)),
        out_shape=jax.ShapeDtypeStruct((depth, MOD_ROWS, n), F32),
        compiler_params=_cparams(("arbitrary", "arbitrary")),
        name="ada",
    )(cc, ada_w, ada_b.reshape(depth, 1, n))


def _mod_row(mod_ref, row, part):
    return mod_ref[pl.ds(row, 1), pl.ds(part * D_MODEL, D_MODEL)]


def _modulate(x, g, shift, scale):
    ms = jnp.mean(x * x, axis=-1, keepdims=True)
    return x * lax.rsqrt(ms + EPS) * g * (1.0 + scale) + shift


SLABS = D_MODEL // 128


def _store_slabs(ref, value):
    n = value.shape[0]
    for s in range(SLABS):
        ref[pl.ds(s, n, stride=SLABS), :] = value[:, 128 * s:128 * (s + 1)]


def _load_slabs(ref, n):
    return jnp.concatenate([ref[pl.ds(s, n, stride=SLABS), :] for s in range(SLABS)], axis=1)


def _head_rms(t, blockdiag, g):
    t2 = t * t
    hi = t2.astype(BF16)
    lo = (t2 - hi.astype(F32)).astype(BF16)
    ms = (jnp.dot(hi, blockdiag, preferred_element_type=F32)
          + jnp.dot(lo, blockdiag, preferred_element_type=F32))
    return t * lax.rsqrt(ms + EPS) * g


def _rope(t, cos, sin_signed):
    width = t.shape[-1]
    lane = lax.broadcasted_iota(jnp.int32, t.shape, 1)
    first = (lane % HEAD_DIM) < (HEAD_DIM // 2)
    partner = jnp.where(first, pltpu.roll(t, width - HEAD_DIM // 2, 1), pltpu.roll(t, HEAD_DIM // 2, 1))
    return t * cos + partner * sin_signed


def _inproj_kernel(x_ref, mod_ref, g_ref, w_ref, b_ref, cos_ref, sin_ref, qg_ref, kg_ref, bq_ref, bk_ref,
                   u_ref, qa_ref, qb_ref, k_ref, v_ref, *, tiles_per_batch):
    row = pl.program_id(0) // tiles_per_batch
    hm = _modulate(x_ref[...], g_ref[...], _mod_row(mod_ref, row, 0), _mod_row(mod_ref, row, 1))
    y = _bdot(hm, w_ref[...]) + b_ref[...]
    u_ref[...] = y[:, :CONV_CH] * _sigmoid(y[:, CONV_CH:2 * CONV_CH])
    cos = cos_ref[...]
    sin = sin_ref[...]
    q = _head_rms(y[:, 2 * CONV_CH:KV_COL], bq_ref[...], qg_ref[...])
    q = _rope(q, cos, sin) * (HEAD_DIM ** -0.5)
    lane = lax.broadcasted_iota(jnp.int32, q.shape, 1)
    low = (lane % (2 * HEAD_DIM)) < HEAD_DIM
    qa_ref[...] = jnp.where(low, q, 0.0).astype(BF16)
    qb_ref[...] = jnp.where(low, 0.0, q).astype(BF16)
    k = _head_rms(y[:, KV_COL:KV_COL + KV_DIM], bk_ref[...], kg_ref[...])
    k_ref[...] = _rope(k, cos[:, :KV_DIM], sin[:, :KV_DIM]).astype(BF16)
    v_ref[...] = y[:, KV_COL + KV_DIM:].astype(BF16)


def _inproj(x2, mod, g, w_bf, b, cos_t, sin_t, qg, kg, bq, bk, seq):
    t, d = x2.shape
    tpb = seq // TM
    const = lambda i: (0, 0)
    tile = lambda i: (i, 0)
    pos = lambda i: (i % tpb, 0)
    return pl.pallas_call(
        functools.partial(_inproj_kernel, tiles_per_batch=tpb),
        grid=(t // TM,),
        in_specs=[
            pl.BlockSpec((TM, d), tile),
            pl.BlockSpec(mod.shape, const),
            pl.BlockSpec((1, d), const),
            pl.BlockSpec(w_bf.shape, const),
            pl.BlockSpec((1, EVEN_IN), const),
            pl.BlockSpec((TM, Q_DIM), pos),
            pl.BlockSpec((TM, Q_DIM), pos),
            pl.BlockSpec((1, Q_DIM), const),
            pl.BlockSpec((1, KV_DIM), const),
            pl.BlockSpec((Q_DIM, Q_DIM), const),
            pl.BlockSpec((KV_DIM, KV_DIM), const),
        ],
        out_specs=[
            pl.BlockSpec((TM, CONV_CH), tile),
            pl.BlockSpec((TM, Q_DIM), tile),
            pl.BlockSpec((TM, Q_DIM), tile),
            pl.BlockSpec((TM, KV_DIM), tile),
            pl.BlockSpec((TM, KV_DIM), tile),
        ],
        out_shape=[
            jax.ShapeDtypeStruct((t, CONV_CH), F32),
            jax.ShapeDtypeStruct((t, Q_DIM), BF16),
            jax.ShapeDtypeStruct((t, Q_DIM), BF16),
            jax.ShapeDtypeStruct((t, KV_DIM), BF16),
            jax.ShapeDtypeStruct((t, KV_DIM), BF16),
        ],
        compiler_params=_cparams(("arbitrary",)),
        name="inproj",
    )(x2, mod, g, w_bf, b, cos_t, sin_t, qg, kg, bq, bk)


def _ctx_kv_kernel(g_in_ref, mod_ref, g_ref, w_ref, b_ref, kg_ref, bk_ref, k_ref, v_ref, *, mod_row):
    gm = _modulate(g_in_ref[...], g_ref[...], _mod_row(mod_ref, mod_row, 0), _mod_row(mod_ref, mod_row, 1))
    y = _bdot(gm, w_ref[...]) + b_ref[...]
    k_ref[...] = _head_rms(y[:, :KV_DIM], bk_ref[...], kg_ref[...]).astype(BF16)
    v_ref[...] = y[:, KV_DIM:].astype(BF16)


def _ctx_kv(ctx2, mod, g, w_kv_bf, b_kv, kg, bk, mod_row):
    t, d = ctx2.shape
    const = lambda i: (0, 0)
    tile = lambda i: (i, 0)
    return pl.pallas_call(
        functools.partial(_ctx_kv_kernel, mod_row=mod_row),
        grid=(t // TM,),
        in_specs=[
            pl.BlockSpec((TM, d), tile),
            pl.BlockSpec(mod.shape, const),
            pl.BlockSpec((1, d), const),
            pl.BlockSpec(w_kv_bf.shape, const),
            pl.BlockSpec((1, 2 * KV_DIM), const),
            pl.BlockSpec((1, KV_DIM), const),
            pl.BlockSpec((KV_DIM, KV_DIM), const),
        ],
        out_specs=[pl.BlockSpec((TM, KV_DIM), tile), pl.BlockSpec((TM, KV_DIM), tile)],
        out_shape=[jax.ShapeDtypeStruct((t, KV_DIM), BF16), jax.ShapeDtypeStruct((t, KV_DIM), BF16)],
        compiler_params=_cparams(("arbitrary",)),
        name="ctx_kv",
    )(ctx2, mod, g, w_kv_bf, b_kv, kg, bk)


CONV_ROWS = 64


def _conv_kernel(u_ref, w_ref, cb_ref, lg_ref, lb_ref, o_ref, pad_ref):
    seq = u_ref.shape[1]
    zeros = jnp.zeros((CONV_HALO, CONV_CH), F32)
    pad_ref[pl.ds(0, CONV_HALO), :] = zeros
    pad_ref[pl.ds(CONV_HALO, seq), :] = u_ref[0]
    pad_ref[pl.ds(CONV_HALO + seq, CONV_HALO), :] = zeros
    first = CONV_HALO - CONV_WIDTH // 2

    def body(i, carry):
        t0 = pl.multiple_of(i * CONV_ROWS, CONV_ROWS)
        chunks = []
        for c in range(CONV_CH // 128):
            cols = slice(128 * c, 128 * (c + 1))
            win = pad_ref[pl.ds(t0, CONV_ROWS + 2 * CONV_HALO), cols]
            acc = jnp.zeros((CONV_ROWS, 128), F32) + cb_ref[:, cols]
            for r in range(8):
                rows = CONV_ROWS if r == 0 else CONV_ROWS + 8
                z = None
                for a in range((CONV_WIDTH + first + 7) // 8):
                    tap = 8 * a + r - first
                    if 0 <= tap < CONV_WIDTH:
                        term = w_ref[tap:tap + 1, cols] * win[8 * a:8 * a + rows, :]
                        z = term if z is None else z + term
                acc = acc + z[r:r + CONV_ROWS, :]
            chunks.append(acc)
        acc = jnp.concatenate(chunks, axis=-1)
        mu = jnp.mean(acc, axis=-1, keepdims=True)
        cen = acc - mu
        var = jnp.mean(cen * cen, axis=-1, keepdims=True)
        yn = cen * lax.rsqrt(var + EPS) * lg_ref[...] + lb_ref[...]
        o_ref[0, pl.ds(t0, CONV_ROWS), :] = (yn * _sigmoid(yn)).astype(BF16)
        return carry

    lax.fori_loop(0, seq // CONV_ROWS, body, 0)


def _conv(u3, w_pad, cb, lg, lb):
    bsz, seq, ch = u3.shape
    const = lambda b: (0, 0)
    return pl.pallas_call(
        _conv_kernel,
        grid=(bsz,),
        in_specs=[
            pl.BlockSpec((1, seq, ch), lambda b: (b, 0, 0)),
            pl.BlockSpec(w_pad.shape, const),
            pl.BlockSpec((1, ch), const),
            pl.BlockSpec((1, ch), const),
            pl.BlockSpec((1, ch), const),
        ],
        out_specs=pl.BlockSpec((1, seq, ch), lambda b: (b, 0, 0)),
        out_shape=jax.ShapeDtypeStruct((bsz, seq, ch), BF16),
        scratch_shapes=[pltpu.VMEM((seq + 2 * CONV_HALO, ch), F32)],
        compiler_params=_cparams(("arbitrary",)),
        name="conv",
    )(u3, w_pad, cb, lg, lb)


def _attn_kernel(sink_ref, qa_ref, qb_ref, kp_ref, kc_ref, kn_ref, kx_ref, vp_ref, vc_ref, vn_ref, vx_ref, o_ref):
    n = pl.program_id(1)
    nb = pl.num_programs(1)
    kcat = jnp.concatenate([kp_ref[0], kc_ref[0], kn_ref[0], kx_ref[0]], axis=0)
    vcat = jnp.concatenate([vp_ref[0], vc_ref[0], vn_ref[0], vx_ref[0]], axis=0)
    blk = ATT_BLOCK
    qi = lax.broadcasted_iota(jnp.int32, (blk, blk), 0)
    kj = lax.broadcasted_iota(jnp.int32, (blk, blk), 1)
    mask_prev = (kj >= qi) & (n > 0)
    mask_next = (kj <= qi) & (n < nb - 1)
    n_chunks = Q_DIM // (2 * HEAD_DIM)
    res = []
    for kvh, q_ref in enumerate((qa_ref, qb_ref)):
        qv = q_ref[0]
        q4 = jnp.concatenate([qv[:, 128 * j:128 * (j + 1)] for j in range(n_chunks)], axis=0)
        s = lax.dot_general(q4, kcat, (((1,), (1,)), ((), ())), preferred_element_type=F32)
        outs = []
        for j in range(n_chunks):
            sj = s[blk * j:blk * (j + 1)]
            sp = jnp.where(mask_prev, sj[:, :blk], NEG_INF)
            sc = sj[:, blk:2 * blk]
            sn = jnp.where(mask_next, sj[:, 2 * blk:3 * blk], NEG_INF)
            sx = [sj[:, c0:c0 + blk] for c0 in range(3 * blk, sj.shape[1], blk)]
            sink = sink_ref[kvh * n_chunks + j]
            lanes = [sp, sc, sn] + sx
            m = jnp.maximum(jnp.max(functools.reduce(jnp.maximum, lanes), axis=-1, keepdims=True), sink)
            parts = [jnp.exp(t - m) for t in lanes]
            den = jnp.sum(functools.reduce(jnp.add, parts), axis=-1, keepdims=True) + jnp.exp(sink - m)
            pcat = jnp.concatenate(parts, axis=-1).astype(BF16)
            o = jnp.dot(pcat, vcat, preferred_element_type=F32)
            outs.append(o * (1.0 / den))
        res.append(outs)
    lane = lax.broadcasted_iota(jnp.int32, (blk, 2 * HEAD_DIM), 1)
    low = lane < HEAD_DIM
    o_ref[0] = jnp.concatenate([jnp.where(low, res[0][j], res[1][j]) for j in range(n_chunks)],
                               axis=-1).astype(BF16)


def _attention(sink, qa, qb, k, v, kx, vx):
    bsz, seq, _ = qa.shape
    nb = seq // ATT_BLOCK
    ctx_len = kx.shape[1]
    cur = lambda b, n, s: (b, n, 0)
    prev = lambda b, n, s: (b, jnp.maximum(n - 1, 0), 0)
    nxt = lambda b, n, s: (b, jnp.minimum(n + 1, nb - 1), 0)
    whole = lambda b, n, s: (b, 0, 0)
    kvspec = lambda im: pl.BlockSpec((1, ATT_BLOCK, KV_DIM), im)
    return pl.pallas_call(
        _attn_kernel,
        grid_spec=pltpu.PrefetchScalarGridSpec(
            num_scalar_prefetch=1,
            grid=(bsz, nb),
            in_specs=[
                pl.BlockSpec((1, ATT_BLOCK, Q_DIM), cur),
                pl.BlockSpec((1, ATT_BLOCK, Q_DIM), cur),
                kvspec(prev), kvspec(cur), kvspec(nxt),
                pl.BlockSpec((1, ctx_len, KV_DIM), whole),
                kvspec(prev), kvspec(cur), kvspec(nxt),
                pl.BlockSpec((1, ctx_len, KV_DIM), whole),
            ],
            out_specs=pl.BlockSpec((1, ATT_BLOCK, Q_DIM), cur),
        ),
        out_shape=jax.ShapeDtypeStruct((bsz, seq, Q_DIM), BF16),
        compiler_params=_cparams(("arbitrary", "arbitrary")),
        name="attention",
    )(sink, qa, qb, k, k, k, kx, v, v, v, vx)


def _erf(x):
    return lax.erf(x)


def _gmlp_kernel(x_ref, mod_ref, g_ref, w_ref, b_ref, lg_ref, lb_ref, ws_ref, bst_ref, a_ref, *, tiles_per_batch):
    row = pl.program_id(0) // tiles_per_batch
    hm = _modulate(x_ref[...], g_ref[...], _mod_row(mod_ref, row, 0), _mod_row(mod_ref, row, 1))
    z = _bdot(hm, w_ref[...]) + b_ref[...]
    z = 0.5 * z * (1.0 + _erf(z * (1.0 / math.sqrt(2.0))))
    half = z.shape[1] // 2
    u = z[:, :half]
    v = z[:, half:]
    mu = jnp.mean(v, axis=-1, keepdims=True)
    cen = v - mu
    var = jnp.mean(cen * cen, axis=-1, keepdims=True)
    vn = (cen * lax.rsqrt(var + EPS) * lg_ref[...] + lb_ref[...]).astype(BF16)
    gch = half // GMLP_GROUPS
    for c in range(x_ref.shape[0] // CHUNK):
        rows = slice(c * CHUNK, (c + 1) * CHUNK)
        for g in range(GMLP_GROUPS):
            cols = slice(g * gch, (g + 1) * gch)
            sv = jnp.dot(ws_ref[g], vn[rows, cols], preferred_element_type=F32) + bst_ref[:, g:g + 1]
            a_ref[rows, cols] = (u[rows, cols] * sv).astype(BF16)


def _gmlp(x2, mod, g, w_bf, b, lg, lb, ws_bf, bst, seq):
    t, d = x2.shape
    tpb = seq // TM
    const = lambda i: (0, 0)
    tile = lambda i: (i, 0)
    n_in = w_bf.shape[1]
    return pl.pallas_call(
        functools.partial(_gmlp_kernel, tiles_per_batch=tpb),
        grid=(t // TM,),
        in_specs=[
            pl.BlockSpec((TM, d), tile),
            pl.BlockSpec(mod.shape, const),
            pl.BlockSpec((1, d), const),
            pl.BlockSpec(w_bf.shape, const),
            pl.BlockSpec((1, n_in), const),
            pl.BlockSpec((1, n_in // 2), const),
            pl.BlockSpec((1, n_in // 2), const),
            pl.BlockSpec(ws_bf.shape, lambda i: (0, 0, 0)),
            pl.BlockSpec(bst.shape, const),
        ],
        out_specs=pl.BlockSpec((TM, n_in // 2), tile),
        out_shape=jax.ShapeDtypeStruct((t, n_in // 2), BF16),
        compiler_params=_cparams(("arbitrary",)),
        name="gmlp",
    )(x2, mod, g, w_bf, b, lg, lb, ws_bf, bst)


def _post_kernel(*refs, n_parts, tiles_per_batch):
    a_refs = refs[:n_parts]
    w_refs = refs[n_parts:2 * n_parts]
    bo_ref, x_ref, mod_ref, gf_ref, rwt_ref, rb_ref, h1_ref, hf_ref, idx_ref, gate_ref = refs[2 * n_parts:]
    row = pl.program_id(0) // tiles_per_batch
    mix = bo_ref[...]
    for a_ref, w_ref in zip(a_refs, w_refs):
        mix = mix + jnp.dot(a_ref[...], w_ref[...], preferred_element_type=F32)
    h1 = x_ref[...] + _mod_row(mod_ref, row, 2) * mix
    h1_ref[...] = h1
    hf = _modulate(h1, gf_ref[...], _mod_row(mod_ref, row, 3), _mod_row(mod_ref, row, 4))
    _store_slabs(hf_ref, hf)
    logits = lax.dot_general(rwt_ref[...], hf.astype(BF16), (((1,), (1,)), ((), ())),
                             preferred_element_type=F32) + rb_ref[...]
    eid = lax.broadcasted_iota(jnp.int32, logits.shape, 0)
    vals, idxs = [], []
    for _ in range(TOP_K):
        m = jnp.max(logits, axis=0, keepdims=True)
        pick = jnp.min(jnp.where(logits == m, eid, N_EXPERTS), axis=0, keepdims=True)
        vals.append(m)
        idxs.append(pick)
        logits = jnp.where(eid == pick, -jnp.inf, logits)
    exps = [jnp.exp(v - vals[0]) for v in vals]
    inv = 1.0 / sum(exps)
    tm = logits.shape[1]
    idx_ref[0] = jnp.concatenate(idxs + [jnp.zeros((8 - TOP_K, tm), jnp.int32)], axis=0)
    gate_ref[0] = jnp.concatenate([e * inv for e in exps] + [jnp.zeros((8 - TOP_K, tm), F32)], axis=0)


def _post(a_parts, w_parts, b_out, x2, mod, gf, rwt_bf, rb_col, seq):
    t, d = x2.shape
    tpb = seq // TM
    nt = t // TM
    const = lambda i: (0, 0)
    tile = lambda i: (i, 0)
    in_specs = ([pl.BlockSpec((TM, a.shape[1]), tile) for a in a_parts]
                + [pl.BlockSpec(w.shape, const) for w in w_parts]
                + [pl.BlockSpec((1, d), const), pl.BlockSpec((TM, d), tile), pl.BlockSpec(mod.shape, const),
                   pl.BlockSpec((1, d), const), pl.BlockSpec(rwt_bf.shape, const),
                   pl.BlockSpec(rb_col.shape, const)])
    return pl.pallas_call(
        functools.partial(_post_kernel, n_parts=len(a_parts), tiles_per_batch=tpb),
        grid=(nt,),
        in_specs=in_specs,
        out_specs=[
            pl.BlockSpec((TM, d), tile),
            pl.BlockSpec((TM * SLABS, 128), tile),
            pl.BlockSpec((1, 8, TM), lambda i: (i, 0, 0)),
            pl.BlockSpec((1, 8, TM), lambda i: (i, 0, 0)),
        ],
        out_shape=[
            jax.ShapeDtypeStruct((t, d), F32),
            jax.ShapeDtypeStruct((t * SLABS, 128), F32),
            jax.ShapeDtypeStruct((nt, 8, TM), jnp.int32),
            jax.ShapeDtypeStruct((nt, 8, TM), F32),
        ],
        compiler_params=_cparams(("arbitrary",)),
        name="post",
    )(*a_parts, *w_parts, b_out, x2, mod, gf, rwt_bf, rb_col)


META_LANES = 128


def _max_blocks(n_tokens):
    return (n_tokens * TOP_K + N_EXPERTS * (EXPERT_BM - 1) + EXPERT_BM - 1) // EXPERT_BM


def _route_kernel(idx_ref, dest_ref, meta_ref, pfull_ref):
    nt, _, tc = idx_ref.shape
    e = N_EXPERTS
    eid = lax.broadcasted_iota(jnp.int32, (e, tc), 0)
    r = lax.broadcasted_iota(jnp.int32, (tc, tc), 0)
    c = lax.broadcasted_iota(jnp.int32, (tc, tc), 1)
    upper = (r < c).astype(BF16)

    def onehots(ci):
        blk = idx_ref[ci]
        return [eid == blk[k:k + 1, :] for k in range(TOP_K)]

    def count_body(ci, carry):
        oh = onehots(ci)
        member = sum(o.astype(F32) for o in oh)
        pfull_ref[ci] = jnp.dot(member.astype(BF16), upper, preferred_element_type=F32) + carry
        return carry + jnp.sum(member, axis=1, keepdims=True)

    counts = lax.fori_loop(0, nt, count_body, jnp.zeros((e, 1), F32))
    nblk = jnp.floor((counts + (EXPERT_BM - 1)) * (1.0 / EXPERT_BM))
    rr = lax.broadcasted_iota(jnp.int32, (e, e), 0)
    cc = lax.broadcasted_iota(jnp.int32, (e, e), 1)
    lower = (cc <= rr).astype(BF16)
    pend_b = jnp.dot(lower, jnp.broadcast_to(nblk, (e, 128)).astype(BF16),
                     preferred_element_type=F32)[:, 0:1]
    pstart = (pend_b - nblk) * EXPERT_BM

    def dest_body(ci, carry):
        oh = onehots(ci)
        base = pfull_ref[ci] + pstart
        rows = [jnp.sum(jnp.where(o, base, 0.0), axis=0, keepdims=True) for o in oh]
        dest_ref[ci] = jnp.concatenate(rows + [jnp.zeros((8 - TOP_K, tc), F32)], axis=0).astype(jnp.int32)
        return carry

    lax.fori_loop(0, nt, dest_body, 0)
    lanes = meta_ref.shape[1]
    bid = lax.broadcasted_iota(jnp.int32, (e, lanes), 1)
    sub = lax.broadcasted_iota(jnp.int32, (e, lanes), 0)
    n_used = jnp.broadcast_to(pend_b[e - 1:e, :], (1, lanes))
    to_lanes = lambda col: jnp.sum(jnp.where(sub == bid, col, 0.0), axis=0, keepdims=True)
    meta_ref[...] = jnp.concatenate([n_used, to_lanes(nblk), to_lanes(pend_b - nblk),
                                     jnp.zeros((5, lanes), F32)], axis=0).astype(jnp.int32)


def _route(idx3):
    nt, _, tc = idx3.shape
    return pl.pallas_call(
        _route_kernel,
        out_shape=[jax.ShapeDtypeStruct((nt, 8, tc), jnp.int32),
                   jax.ShapeDtypeStruct((8, META_LANES), jnp.int32)],
        scratch_shapes=[pltpu.VMEM((nt, N_EXPERTS, tc), F32)],
        compiler_params=pltpu.CompilerParams(vmem_limit_bytes=VMEM_LIMIT),
        name="route",
    )(idx3)


def _invert_kernel(dest_ref, src_ref, zeros_ref, map_ref, sem):
    tm = dest_ref.shape[2]
    step = pl.program_id(0)

    @pl.when(step == 0)
    def _():
        zeros_ref[...] = jnp.zeros(zeros_ref.shape, zeros_ref.dtype)
        cp = pltpu.make_async_copy(zeros_ref, map_ref, sem)
        cp.start()
        cp.wait()

    def body(j, carry):
        for k in range(TOP_K):
            map_ref[dest_ref[0, k, j]] = (step * tm + j) * TOP_K + k
        return carry

    lax.fori_loop(0, tm, body, 0, unroll=8)

    @pl.when(step == pl.num_programs(0) - 1)
    def _():
        cp = pltpu.make_async_copy(map_ref, src_ref, sem)
        cp.start()
        cp.wait()


def _invert(dest3, n_rows):
    nt = dest3.shape[0]
    return pl.pallas_call(
        _invert_kernel,
        grid=(nt,),
        in_specs=[pl.BlockSpec((1, 8, TM), lambda i: (i, 0, 0), memory_space=pltpu.SMEM)],
        out_specs=pl.BlockSpec(memory_space=pl.ANY),
        out_shape=jax.ShapeDtypeStruct((n_rows,), jnp.int32),
        scratch_shapes=[pltpu.VMEM((n_rows,), jnp.int32), pltpu.SMEM((n_rows,), jnp.int32),
                        pltpu.SemaphoreType.DMA],
        compiler_params=_cparams(("arbitrary",)),
        name="invert",
    )(dest3)


BLOCK_SLAB_ROWS = EXPERT_BM * SLABS
GATHER_AHEAD = 2
GATHER_RING = GATHER_AHEAD + 1


def _expert_kernel(nu_ref, nblk_ref, bstart_ref, src_ref, hf_ref, w1_ref, b1_ref, w2_ref, b2_ref, y_ref,
                   w1b_ref, w2b_ref, xbuf_ref, ybuf_ref, gsem, ysem):
    ex = pl.program_id(0)
    n_used = nu_ref[0]
    n_mine = nblk_ref[ex]
    first = bstart_ref[ex]

    def row_gather(g, j):
        block = jnp.minimum(g, n_used - 1)
        slot = lax.rem(g, GATHER_RING)
        tok = lax.shift_right_logical(src_ref[block * EXPERT_BM + j], 2)
        return pltpu.make_async_copy(hf_ref.at[pl.ds(pl.multiple_of(tok * SLABS, SLABS), SLABS)],
                                     xbuf_ref.at[slot, pl.ds(j * SLABS, SLABS)], gsem.at[slot])

    def start_gather(g):
        for j in range(EXPERT_BM):
            row_gather(g, j).start(priority=1)

    def wait_gather(g):
        for j in range(EXPERT_BM):
            row_gather(g, j).wait()

    def y_copy(block, slot):
        rows = pl.ds(pl.multiple_of(block * BLOCK_SLAB_ROWS, BLOCK_SLAB_ROWS), BLOCK_SLAB_ROWS)
        return pltpu.make_async_copy(ybuf_ref.at[slot], y_ref.at[rows], ysem.at[slot])

    @pl.when(ex == 0)
    def _():
        for g in range(GATHER_AHEAD):
            start_gather(g)

    @pl.when(n_mine > 0)
    def _():
        w1b_ref[...] = w1_ref[0, 0].astype(BF16)
        w2b_ref[...] = w2_ref[0, 0].astype(BF16)

    def block_body(i, carry):
        b = first + i
        slot = lax.rem(b, 2)

        @pl.when(b >= 2)
        def _():
            y_copy(b - 2, slot).wait()

        wait_gather(b)
        x = _load_slabs(xbuf_ref.at[lax.rem(b, GATHER_RING)], EXPERT_BM).astype(BF16)
        start_gather(b + GATHER_AHEAD)
        h = jnp.dot(x, w1b_ref[...], preferred_element_type=F32) + b1_ref[0, 0]
        gate = jnp.minimum(h[:, :EXPERT_FF], SWIGLU_LIMIT)
        up = jnp.clip(h[:, EXPERT_FF:], -SWIGLU_LIMIT, SWIGLU_LIMIT)
        act = (up + 1.0) * gate * _sigmoid(gate * SWIGLU_ALPHA)
        _store_slabs(ybuf_ref.at[slot],
                     jnp.dot(act.astype(BF16), w2b_ref[...], preferred_element_type=F32) + b2_ref[0, 0])
        y_copy(b, slot).start()
        return carry

    lax.fori_loop(0, n_mine, block_body, 0)

    @pl.when(ex == pl.num_programs(0) - 1)
    def _():
        last = n_used - 1
        spare = lax.rem(n_used, 2)
        for g in range(GATHER_AHEAD):
            wait_gather(n_used + g)
        y_copy(last, lax.rem(last, 2)).wait()

        @pl.when(n_used >= 2)
        def _():
            y_copy(last - 1, spare).wait()

        ybuf_ref[0] = jnp.zeros(ybuf_ref.shape[1:], ybuf_ref.dtype)
        n_total = y_ref.shape[0] // BLOCK_SLAB_ROWS

        def tail_start(b, carry):
            y_copy(b, 0).start()
            return carry

        def tail_wait(b, carry):
            y_copy(b, 0).wait()
            return carry

        lax.fori_loop(n_used, n_total, tail_start, 0)
        lax.fori_loop(n_used, n_total, tail_wait, 0)


def _experts(layer, n_used, nblk, bstart, src, hf2, w1, b1, w2, b2):
    n_rows = src.shape[0]
    _, n_exp, d, ff2 = w1.shape
    sel = lambda e, *_: (layer, e, 0, 0)
    return pl.pallas_call(
        _expert_kernel,
        grid_spec=pltpu.PrefetchScalarGridSpec(
            num_scalar_prefetch=4,
            grid=(n_exp,),
            in_specs=[
                pl.BlockSpec(memory_space=pl.ANY),
                pl.BlockSpec((1, 1, d, ff2), sel),
                pl.BlockSpec((1, 1, 1, ff2), sel),
                pl.BlockSpec((1, 1, ff2 // 2, d), sel),
                pl.BlockSpec((1, 1, 1, d), sel),
            ],
            out_specs=pl.BlockSpec(memory_space=pl.ANY),
            scratch_shapes=[
                pltpu.VMEM((d, ff2), BF16),
                pltpu.VMEM((ff2 // 2, d), BF16),
                pltpu.VMEM((GATHER_RING, BLOCK_SLAB_ROWS, 128), F32),
                pltpu.VMEM((2, BLOCK_SLAB_ROWS, 128), F32),
                pltpu.SemaphoreType.DMA((GATHER_RING,)),
                pltpu.SemaphoreType.DMA((2,)),
            ],
        ),
        out_shape=jax.ShapeDtypeStruct((n_rows * SLABS, 128), F32),
        compiler_params=pltpu.CompilerParams(dimension_semantics=("arbitrary",),
                                             vmem_limit_bytes=EXPERT_VMEM_LIMIT),
        name="experts",
    )(n_used, nblk, bstart, src, hf2, w1, b1.reshape(b1.shape[0], n_exp, 1, ff2), w2,
      b2.reshape(b2.shape[0], n_exp, 1, d))


def _combine_kernel(dest_ref, gate_ref, h1_ref, mod_ref, y_ref, o_ref, ybuf_ref, sem, *, tiles_per_batch):
    tm = h1_ref.shape[0]
    row = pl.program_id(0) // tiles_per_batch

    def row_copy(j, k):
        return pltpu.make_async_copy(y_ref.at[dest_ref[0, k, j]],
                                     ybuf_ref.at[k, pl.ds(pl.multiple_of(j * SLABS, SLABS), SLABS)], sem)

    def start_body(j, carry):
        for k in range(TOP_K):
            row_copy(j, k).start(priority=k % 2)
        return carry

    def wait_body(j, carry):
        for k in range(TOP_K):
            row_copy(j, k).wait()
        return carry

    lax.fori_loop(0, tm, start_body, 0, unroll=8)
    lax.fori_loop(0, tm, wait_body, 0, unroll=8)
    gates = gate_ref[0].T
    acc = gates[:, 0:1] * _load_slabs(ybuf_ref.at[0], tm)
    for k in range(1, TOP_K):
        acc = acc + gates[:, k:k + 1] * _load_slabs(ybuf_ref.at[k], tm)
    o_ref[...] = h1_ref[...] + _mod_row(mod_ref, row, 5) * acc


def _combine(dest3, gates3, h1, mod, y_rows, seq):
    t, d = h1.shape
    tpb = seq // TM
    return pl.pallas_call(
        functools.partial(_combine_kernel, tiles_per_batch=tpb),
        grid=(t // TM,),
        in_specs=[
            pl.BlockSpec((1, 8, TM), lambda i: (i, 0, 0), memory_space=pltpu.SMEM),
            pl.BlockSpec((1, 8, TM), lambda i: (i, 0, 0)),
            pl.BlockSpec((TM, d), lambda i: (i, 0)),
            pl.BlockSpec(mod.shape, lambda i: (0, 0)),
            pl.BlockSpec(memory_space=pl.ANY),
        ],
        out_specs=pl.BlockSpec((TM, d), lambda i: (i, 0)),
        out_shape=jax.ShapeDtypeStruct((t, d), F32),
        scratch_shapes=[pltpu.VMEM((TOP_K, TM * SLABS, 128), F32), pltpu.SemaphoreType.DMA],
        compiler_params=_cparams(("arbitrary",)),
        name="combine",
    )(dest3, gates3, h1, mod, y_rows)


def _moe(layer, h1, hf2, idx3, gates3, mod, w1, b1, w2, b2, seq):
    dest3, meta = _route(idx3)
    src = _invert(dest3, _max_blocks(h1.shape[0]) * EXPERT_BM)
    y2 = _experts(layer, meta[0, :1], meta[1, :N_EXPERTS], meta[2, :N_EXPERTS], src, hf2, w1, b1, w2, b2)
    return _combine(dest3, gates3, h1, mod, y2.reshape(-1, SLABS, 128), seq)


def _rope_tables(seq):
    rows = seq // GRID_W
    r = jnp.repeat(jnp.arange(rows), GRID_W).astype(F32)
    col = jnp.tile(jnp.arange(GRID_W), rows).astype(F32)
    pairs = HEAD_DIM // 4
    inv_freq = ROPE_THETA ** (-jnp.arange(pairs, dtype=F32) / pairs)
    ang = jnp.concatenate([r[:, None] * inv_freq, col[:, None] * inv_freq], axis=-1)
    cos, sin = jnp.cos(ang), jnp.sin(ang)
    cos_t = jnp.tile(cos, (1, Q_DIM // (HEAD_DIM // 2)))
    sin_t = jnp.tile(jnp.concatenate([-sin, sin], axis=-1), (1, N_Q_HEADS))
    return cos_t, sin_t


def _block_diag_mean(width):
    i = jnp.arange(width)
    return jnp.where((i[:, None] // HEAD_DIM) == (i[None, :] // HEAD_DIM), 1.0 / HEAD_DIM, 0.0).astype(BF16)


def _head_perm():
    per_kv = N_Q_HEADS // N_KV_HEADS
    heads = []
    for j in range(per_kv):
        heads += [j, per_kv + j]
    return jnp.concatenate([jnp.arange(HEAD_DIM) + h * HEAD_DIM for h in heads])


def kernel(x, c, ctx, c_ctx, ada_w, ada_b, norm_mix_g, norm_ffn_g, ev_w_in, ev_b_in, ev_conv_w, ev_conv_b, ev_conv_ln_g, ev_conv_ln_b, ev_q_norm_g, ev_k_norm_g, ev_sink, ev_w_out, ev_b_out, od_w_in, od_b_in, od_v_ln_g, od_v_ln_b, od_w_s, od_b_s, od_w_out, od_b_out, moe_router_w, moe_router_b, moe_w1, moe_b1, moe_w2, moe_b2):
    bsz, seq, d = x.shape
    ctx_len = ctx.shape[1]
    assert d == D_MODEL and seq % TM == 0 and (bsz * ctx_len) % TM == 0 and bsz < MOD_ROWS
    assert ctx_len % ATT_BLOCK == 0 and seq % GRID_W == 0
    assert ada_w.shape[0] == 2, "layer 0 is the conv/attention layer, layer 1 the gMLP layer"
    ctx_row = bsz
    cc = jnp.zeros((MOD_ROWS, d), F32).at[:bsz].set(c).at[ctx_row].set(c_ctx)
    mods = _ada(cc, ada_w, ada_b)
    row2 = lambda v: v.reshape(1, -1)

    perm = _head_perm()
    w_in = ev_w_in[0]
    w_in_bf = jnp.concatenate([w_in[:, :2 * CONV_CH], w_in[:, 2 * CONV_CH:KV_COL][:, perm], w_in[:, KV_COL:]],
                              axis=1).astype(BF16)
    b_in = ev_b_in[0]
    b_in_p = row2(jnp.concatenate([b_in[:2 * CONV_CH], b_in[2 * CONV_CH:KV_COL][perm], b_in[KV_COL:]]))
    cos_t, sin_t = _rope_tables(seq)
    qg = row2(jnp.tile(ev_q_norm_g[0], N_Q_HEADS))
    kg = row2(jnp.tile(ev_k_norm_g[0], N_KV_HEADS))
    bq, bk = _block_diag_mean(Q_DIM), _block_diag_mean(KV_DIM)
    x2 = x.reshape(bsz * seq, d)
    g_mix0 = row2(norm_mix_g[0])
    u, qa, qb, k, v = _inproj(x2, mods[0], g_mix0, w_in_bf, b_in_p, cos_t, sin_t, qg, kg, bq, bk, seq)
    kx, vx = _ctx_kv(ctx.reshape(bsz * ctx_len, d), mods[0], g_mix0, w_in_bf[:, KV_COL:], b_in_p[:, KV_COL:],
                     kg, bk, ctx_row)
    conv_w = jnp.zeros((32, CONV_CH), F32).at[:CONV_WIDTH].set(ev_conv_w[0])
    conv_h = _conv(u.reshape(bsz, seq, CONV_CH), conv_w, row2(ev_conv_b[0]), row2(ev_conv_ln_g[0]),
                   row2(ev_conv_ln_b[0]))
    r3 = lambda a, w: a.reshape(bsz, -1, w)
    att = _attention(ev_sink[0], r3(qa, Q_DIM), r3(qb, Q_DIM), r3(k, KV_DIM), r3(v, KV_DIM),
                     r3(kx, KV_DIM), r3(vx, KV_DIM))
    w_out = ev_w_out[0]
    w_out_conv = w_out[:CONV_CH].astype(BF16)
    w_out_att = w_out[CONV_CH:][perm].astype(BF16)

    def router(l):
        return moe_router_w[l].T.astype(BF16), moe_router_b[l].reshape(N_EXPERTS, 1)

    rwt, rb = router(0)
    h1, hf, idx3, gates3 = _post([conv_h.reshape(bsz * seq, CONV_CH), att.reshape(bsz * seq, Q_DIM)],
                                 [w_out_conv, w_out_att], row2(ev_b_out[0]), x2, mods[0],
                                 row2(norm_ffn_g[0]), rwt, rb, seq)
    h = _moe(0, h1, hf, idx3, gates3, mods[0], moe_w1, moe_b1, moe_w2, moe_b2, seq)

    a = _gmlp(h, mods[1], row2(norm_mix_g[1]), od_w_in[0].astype(BF16), row2(od_b_in[0]), row2(od_v_ln_g[0]),
              row2(od_v_ln_b[0]), od_w_s[0].astype(BF16), od_b_s[0].T, seq)
    rwt, rb = router(1)
    h1, hf, idx3, gates3 = _post([a], [od_w_out[0].astype(BF16)], row2(od_b_out[0]), h, mods[1],
                                 row2(norm_ffn_g[1]), rwt, rb, seq)
    h = _moe(1, h1, hf, idx3, gates3, mods[1], moe_w1, moe_b1, moe_w2, moe_b2, seq)
    return h.reshape(bsz, seq, d)
```

```python
import functools
import math

import jax
import jax.numpy as jnp
from jax import lax
from jax.experimental import pallas as pl
from jax.experimental.pallas import tpu as pltpu

F32 = jnp.float32
BF16 = jnp.bfloat16

D_MODEL = 1024
HEAD_DIM = 64
N_Q_HEADS = 8
N_KV_HEADS = 2
Q_DIM = N_Q_HEADS * HEAD_DIM
KV_DIM = N_KV_HEADS * HEAD_DIM
CONV_CH = 512
CONV_WIDTH = 31
CONV_HALO = 16
KV_COL = 2 * CONV_CH + Q_DIM
EVEN_IN = KV_COL + 2 * KV_DIM
ATT_BLOCK = 128
GRID_W = 64
ROPE_THETA = 10000.0
CHUNK = 128
GMLP_GROUPS = 8
N_EXPERTS = 32
TOP_K = 4
EXPERT_FF = 1024
SWIGLU_LIMIT = 7.0
SWIGLU_ALPHA = 1.702
EPS = 1e-6
NEG_INF = -1e30

TM = 512
EXPERT_BM = 256
MOD_ROWS = 16
VMEM_LIMIT = 48 * 1024 * 1024
EXPERT_VMEM_LIMIT = 56 * 1024 * 1024


def _cparams(sem):
    return pltpu.CompilerParams(dimension_semantics=sem, vmem_limit_bytes=VMEM_LIMIT)


def _sigmoid(x):
    return 1.0 / (1.0 + jnp.exp(-x))


def _bdot(a, b):
    return jnp.dot(a.astype(BF16), b.astype(BF16), preferred_element_type=F32)


def _ada_kernel(cc_ref, w_ref, b_ref, o_ref):
    a = cc_ref[...]
    o_ref[0] = _bdot(a * _sigmoid(a), w_ref[0]) + b_ref[0]


def _ada(cc, ada_w, ada_b):
    depth, d, n = ada_w.shape
    tn = 1536
    return pl.pallas_call(
        _ada_kernel,
        grid=(depth, n // tn),
        in_specs=[
            pl.BlockSpec((MOD_ROWS, d), lambda l, j: (0, 0)),
            pl.BlockSpec((1, d, tn), lambda l, j: (l, 0, j)),
            pl.BlockSpec((1, 1, tn), lambda l, j: (l, 0, j)),
        ],
        out_specs=pl.BlockSpec((1, MOD_ROWS, tn), lambda l, j: (l, 0, j)),
        out_shape=jax.ShapeDtypeStruct((depth, MOD_ROWS, n), F32),
        compiler_params=_cparams(("arbitrary", "arbitrary")),
        name="ada",
    )(cc, ada_w, ada_b.reshape(depth, 1, n))


def _mod_row(mod_ref, row, part):
    return mod_ref[pl.ds(row, 1), pl.ds(part * D_MODEL, D_MODEL)]


def _modulate(x, g, shift, scale):
    ms = jnp.mean(x * x, axis=-1, keepdims=True)
    return x * lax.rsqrt(ms + EPS) * g * (1.0 + scale) + shift


SLABS = D_MODEL // 128


def _store_slabs(ref, value):
    n = value.shape[0]
    for s in range(SLABS):
        ref[pl.ds(s, n, stride=SLABS), :] = value[:, 128 * s:128 * (s + 1)]


def _load_slabs(ref, n):
    return jnp.concatenate([ref[pl.ds(s, n, stride=SLABS), :] for s in range(SLABS)], axis=1)


def _head_rms(t, blockdiag, g):
    t2 = t * t
    hi = t2.astype(BF16)
    lo = (t2 - hi.astype(F32)).astype(BF16)
    ms = (jnp.dot(hi, blockdiag, preferred_element_type=F32)
          + jnp.dot(lo, blockdiag, preferred_element_type=F32))
    return t * lax.rsqrt(ms + EPS) * g


def _rope(t, cos, sin_signed):
    width = t.shape[-1]
    lane = lax.broadcasted_iota(jnp.int32, t.shape, 1)
    first = (lane % HEAD_DIM) < (HEAD_DIM // 2)
    partner = jnp.where(first, pltpu.roll(t, width - HEAD_DIM // 2, 1), pltpu.roll(t, HEAD_DIM // 2, 1))
    return t * cos + partner * sin_signed


def _inproj_kernel(x_ref, mod_ref, g_ref, w_ref, b_ref, cos_ref, sin_ref, qg_ref, kg_ref, bq_ref, bk_ref,
                   u_ref, qa_ref, qb_ref, k_ref, v_ref, *, tiles_per_batch):
    row = pl.program_id(0) // tiles_per_batch
    hm = _modulate(x_ref[...], g_ref[...], _mod_row(mod_ref, row, 0), _mod_row(mod_ref, row, 1))
    y = _bdot(hm, w_ref[...]) + b_ref[...]
    u_ref[...] = y[:, :CONV_CH] * _sigmoid(y[:, CONV_CH:2 * CONV_CH])
    cos = cos_ref[...]
    sin = sin_ref[...]
    q = _head_rms(y[:, 2 * CONV_CH:KV_COL], bq_ref[...], qg_ref[...])
    q = _rope(q, cos, sin) * (HEAD_DIM ** -0.5)
    lane = lax.broadcasted_iota(jnp.int32, q.shape, 1)
    low = (lane % (2 * HEAD_DIM)) < HEAD_DIM
    qa_ref[...] = jnp.where(low, q, 0.0).astype(BF16)
    qb_ref[...] = jnp.where(low, 0.0, q).astype(BF16)
    k = _head_rms(y[:, KV_COL:KV_COL + KV_DIM], bk_ref[...], kg_ref[...])
    k_ref[...] = _rope(k, cos[:, :KV_DIM], sin[:, :KV_DIM]).astype(BF16)
    v_ref[...] = y[:, KV_COL + KV_DIM:].astype(BF16)


def _inproj(x2, mod, g, w_bf, b, cos_t, sin_t, qg, kg, bq, bk, seq):
    t, d = x2.shape
    tpb = seq // TM
    const = lambda i: (0, 0)
    tile = lambda i: (i, 0)
    pos = lambda i: (i % tpb, 0)
    return pl.pallas_call(
        functools.partial(_inproj_kernel, tiles_per_batch=tpb),
        grid=(t // TM,),
        in_specs=[
            pl.BlockSpec((TM, d), tile),
            pl.BlockSpec(mod.shape, const),
            pl.BlockSpec((1, d), const),
            pl.BlockSpec(w_bf.shape, const),
            pl.BlockSpec((1, EVEN_IN), const),
            pl.BlockSpec((TM, Q_DIM), pos),
            pl.BlockSpec((TM, Q_DIM), pos),
            pl.BlockSpec((1, Q_DIM), const),
            pl.BlockSpec((1, KV_DIM), const),
            pl.BlockSpec((Q_DIM, Q_DIM), const),
            pl.BlockSpec((KV_DIM, KV_DIM), const),
        ],
        out_specs=[
            pl.BlockSpec((TM, CONV_CH), tile),
            pl.BlockSpec((TM, Q_DIM), tile),
            pl.BlockSpec((TM, Q_DIM), tile),
            pl.BlockSpec((TM, KV_DIM), tile),
            pl.BlockSpec((TM, KV_DIM), tile),
        ],
        out_shape=[
            jax.ShapeDtypeStruct((t, CONV_CH), F32),
            jax.ShapeDtypeStruct((t, Q_DIM), BF16),
            jax.ShapeDtypeStruct((t, Q_DIM), BF16),
            jax.ShapeDtypeStruct((t, KV_DIM), BF16),
            jax.ShapeDtypeStruct((t, KV_DIM), BF16),
        ],
        compiler_params=_cparams(("arbitrary",)),
        name="inproj",
    )(x2, mod, g, w_bf, b, cos_t, sin_t, qg, kg, bq, bk)


def _ctx_kv_kernel(g_in_ref, mod_ref, g_ref, w_ref, b_ref, kg_ref, bk_ref, k_ref, v_ref, *, mod_row):
    gm = _modulate(g_in_ref[...], g_ref[...], _mod_row(mod_ref, mod_row, 0), _mod_row(mod_ref, mod_row, 1))
    y = _bdot(gm, w_ref[...]) + b_ref[...]
    k_ref[...] = _head_rms(y[:, :KV_DIM], bk_ref[...], kg_ref[...]).astype(BF16)
    v_ref[...] = y[:, KV_DIM:].astype(BF16)


def _ctx_kv(ctx2, mod, g, w_kv_bf, b_kv, kg, bk, mod_row):
    t, d = ctx2.shape
    const = lambda i: (0, 0)
    tile = lambda i: (i, 0)
    return pl.pallas_call(
        functools.partial(_ctx_kv_kernel, mod_row=mod_row),
        grid=(t // TM,),
        in_specs=[
            pl.BlockSpec((TM, d), tile),
            pl.BlockSpec(mod.shape, const),
            pl.BlockSpec((1, d), const),
            pl.BlockSpec(w_kv_bf.shape, const),
            pl.BlockSpec((1, 2 * KV_DIM), const),
            pl.BlockSpec((1, KV_DIM), const),
            pl.BlockSpec((KV_DIM, KV_DIM), const),
        ],
        out_specs=[pl.BlockSpec((TM, KV_DIM), tile), pl.BlockSpec((TM, KV_DIM), tile)],
        out_shape=[jax.ShapeDtypeStruct((t, KV_DIM), BF16), jax.ShapeDtypeStruct((t, KV_DIM), BF16)],
        compiler_params=_cparams(("arbitrary",)),
        name="ctx_kv",
    )(ctx2, mod, g, w_kv_bf, b_kv, kg, bk)


CONV_ROWS = 64


def _conv_kernel(u_ref, w_ref, cb_ref, lg_ref, lb_ref, o_ref, pad_ref):
    seq = u_ref.shape[1]
    zeros = jnp.zeros((CONV_HALO, CONV_CH), F32)
    pad_ref[pl.ds(0, CONV_HALO), :] = zeros
    pad_ref[pl.ds(CONV_HALO, seq), :] = u_ref[0]
    pad_ref[pl.ds(CONV_HALO + seq, CONV_HALO), :] = zeros
    first = CONV_HALO - CONV_WIDTH // 2

    def body(i, carry):
        t0 = pl.multiple_of(i * CONV_ROWS, CONV_ROWS)
        chunks = []
        for c in range(CONV_CH // 128):
            cols = slice(128 * c, 128 * (c + 1))
            win = pad_ref[pl.ds(t0, CONV_ROWS + 2 * CONV_HALO), cols]
            acc = jnp.zeros((CONV_ROWS, 128), F32) + cb_ref[:, cols]
            for r in range(8):
                rows = CONV_ROWS if r == 0 else CONV_ROWS + 8
                z = None
                for a in range((CONV_WIDTH + first + 7) // 8):
                    tap = 8 * a + r - first
                    if 0 <= tap < CONV_WIDTH:
                        term = w_ref[tap:tap + 1, cols] * win[8 * a:8 * a + rows, :]
                        z = term if z is None else z + term
                acc = acc + z[r:r + CONV_ROWS, :]
            chunks.append(acc)
        acc = jnp.concatenate(chunks, axis=-1)
        mu = jnp.mean(acc, axis=-1, keepdims=True)
        cen = acc - mu
        var = jnp.mean(cen * cen, axis=-1, keepdims=True)
        yn = cen * lax.rsqrt(var + EPS) * lg_ref[...] + lb_ref[...]
        o_ref[0, pl.ds(t0, CONV_ROWS), :] = (yn * _sigmoid(yn)).astype(BF16)
        return carry

    lax.fori_loop(0, seq // CONV_ROWS, body, 0)


def _conv(u3, w_pad, cb, lg, lb):
    bsz, seq, ch = u3.shape
    const = lambda b: (0, 0)
    return pl.pallas_call(
        _conv_kernel,
        grid=(bsz,),
        in_specs=[
            pl.BlockSpec((1, seq, ch), lambda b: (b, 0, 0)),
            pl.BlockSpec(w_pad.shape, const),
            pl.BlockSpec((1, ch), const),
            pl.BlockSpec((1, ch), const),
            pl.BlockSpec((1, ch), const),
        ],
        out_specs=pl.BlockSpec((1, seq, ch), lambda b: (b, 0, 0)),
        out_shape=jax.ShapeDtypeStruct((bsz, seq, ch), BF16),
        scratch_shapes=[pltpu.VMEM((seq + 2 * CONV_HALO, ch), F32)],
        compiler_params=_cparams(("arbitrary",)),
        name="conv",
    )(u3, w_pad, cb, lg, lb)


def _attn_kernel(sink_ref, qa_ref, qb_ref, kp_ref, kc_ref, kn_ref, kx_ref, vp_ref, vc_ref, vn_ref, vx_ref, o_ref):
    n = pl.program_id(1)
    nb = pl.num_programs(1)
    kcat = jnp.concatenate([kp_ref[0], kc_ref[0], kn_ref[0], kx_ref[0]], axis=0)
    vcat = jnp.concatenate([vp_ref[0], vc_ref[0], vn_ref[0], vx_ref[0]], axis=0)
    blk = ATT_BLOCK
    qi = lax.broadcasted_iota(jnp.int32, (blk, blk), 0)
    kj = lax.broadcasted_iota(jnp.int32, (blk, blk), 1)
    mask_prev = (kj >= qi) & (n > 0)
    mask_next = (kj <= qi) & (n < nb - 1)
    n_chunks = Q_DIM // (2 * HEAD_DIM)
    res = []
    for kvh, q_ref in enumerate((qa_ref, qb_ref)):
        qv = q_ref[0]
        q4 = jnp.concatenate([qv[:, 128 * j:128 * (j + 1)] for j in range(n_chunks)], axis=0)
        s = lax.dot_general(q4, kcat, (((1,), (1,)), ((), ())), preferred_element_type=F32)
        outs = []
        for j in range(n_chunks):
            sj = s[blk * j:blk * (j + 1)]
            sp = jnp.where(mask_prev, sj[:, :blk], NEG_INF)
            sc = sj[:, blk:2 * blk]
            sn = jnp.where(mask_next, sj[:, 2 * blk:3 * blk], NEG_INF)
            sx = [sj[:, c0:c0 + blk] for c0 in range(3 * blk, sj.shape[1], blk)]
            sink = sink_ref[kvh * n_chunks + j]
            lanes = [sp, sc, sn] + sx
            m = jnp.maximum(jnp.max(functools.reduce(jnp.maximum, lanes), axis=-1, keepdims=True), sink)
            parts = [jnp.exp(t - m) for t in lanes]
            den = jnp.sum(functools.reduce(jnp.add, parts), axis=-1, keepdims=True) + jnp.exp(sink - m)
            pcat = jnp.concatenate(parts, axis=-1).astype(BF16)
            o = jnp.dot(pcat, vcat, preferred_element_type=F32)
            outs.append(o * (1.0 / den))
        res.append(outs)
    lane = lax.broadcasted_iota(jnp.int32, (blk, 2 * HEAD_DIM), 1)
    low = lane < HEAD_DIM
    o_ref[0] = jnp.concatenate([jnp.where(low, res[0][j], res[1][j]) for j in range(n_chunks)],
                               axis=-1).astype(BF16)


def _attention(sink, qa, qb, k, v, kx, vx):
    bsz, seq, _ = qa.shape
    nb = seq // ATT_BLOCK
    ctx_len = kx.shape[1]
    cur = lambda b, n, s: (b, n, 0)
    prev = lambda b, n, s: (b, jnp.maximum(n - 1, 0), 0)
    nxt = lambda b, n, s: (b, jnp.minimum(n + 1, nb - 1), 0)
    whole = lambda b, n, s: (b, 0, 0)
    kvspec = lambda im: pl.BlockSpec((1, ATT_BLOCK, KV_DIM), im)
    return pl.pallas_call(
        _attn_kernel,
        grid_spec=pltpu.PrefetchScalarGridSpec(
            num_scalar_prefetch=1,
            grid=(bsz, nb),
            in_specs=[
                pl.BlockSpec((1, ATT_BLOCK, Q_DIM), cur),
                pl.BlockSpec((1, ATT_BLOCK, Q_DIM), cur),
                kvspec(prev), kvspec(cur), kvspec(nxt),
                pl.BlockSpec((1, ctx_len, KV_DIM), whole),
                kvspec(prev), kvspec(cur), kvspec(nxt),
                pl.BlockSpec((1, ctx_len, KV_DIM), whole),
            ],
            out_specs=pl.BlockSpec((1, ATT_BLOCK, Q_DIM), cur),
        ),
        out_shape=jax.ShapeDtypeStruct((bsz, seq, Q_DIM), BF16),
        compiler_params=_cparams(("arbitrary", "arbitrary")),
        name="attention",
    )(sink, qa, qb, k, k, k, kx, v, v, v, vx)


def _erf(x):
    return lax.erf(x)


def _gmlp_kernel(x_ref, mod_ref, g_ref, w_ref, b_ref, lg_ref, lb_ref, ws_ref, bst_ref, a_ref, *, tiles_per_batch):
    row = pl.program_id(0) // tiles_per_batch
    hm = _modulate(x_ref[...], g_ref[...], _mod_row(mod_ref, row, 0), _mod_row(mod_ref, row, 1))
    z = _bdot(hm, w_ref[...]) + b_ref[...]
    z = 0.5 * z * (1.0 + _erf(z * (1.0 / math.sqrt(2.0))))
    half = z.shape[1] // 2
    u = z[:, :half]
    v = z[:, half:]
    mu = jnp.mean(v, axis=-1, keepdims=True)
    cen = v - mu
    var = jnp.mean(cen * cen, axis=-1, keepdims=True)
    vn = (cen * lax.rsqrt(var + EPS) * lg_ref[...] + lb_ref[...]).astype(BF16)
    gch = half // GMLP_GROUPS
    for c in range(x_ref.shape[0] // CHUNK):
        rows = slice(c * CHUNK, (c + 1) * CHUNK)
        for g in range(GMLP_GROUPS):
            cols = slice(g * gch, (g + 1) * gch)
            sv = jnp.dot(ws_ref[g], vn[rows, cols], preferred_element_type=F32) + bst_ref[:, g:g + 1]
            a_ref[rows, cols] = (u[rows, cols] * sv).astype(BF16)


def _gmlp(x2, mod, g, w_bf, b, lg, lb, ws_bf, bst, seq):
    t, d = x2.shape
    tpb = seq // TM
    const = lambda i: (0, 0)
    tile = lambda i: (i, 0)
    n_in = w_bf.shape[1]
    return pl.pallas_call(
        functools.partial(_gmlp_kernel, tiles_per_batch=tpb),
        grid=(t // TM,),
        in_specs=[
            pl.BlockSpec((TM, d), tile),
            pl.BlockSpec(mod.shape, const),
            pl.BlockSpec((1, d), const),
            pl.BlockSpec(w_bf.shape, const),
            pl.BlockSpec((1, n_in), const),
            pl.BlockSpec((1, n_in // 2), const),
            pl.BlockSpec((1, n_in // 2), const),
            pl.BlockSpec(ws_bf.shape, lambda i: (0, 0, 0)),
            pl.BlockSpec(bst.shape, const),
        ],
        out_specs=pl.BlockSpec((TM, n_in // 2), tile),
        out_shape=jax.ShapeDtypeStruct((t, n_in // 2), BF16),
        compiler_params=_cparams(("arbitrary",)),
        name="gmlp",
    )(x2, mod, g, w_bf, b, lg, lb, ws_bf, bst)


def _post_kernel(*refs, n_parts, tiles_per_batch):
    a_refs = refs[:n_parts]
    w_refs = refs[n_parts:2 * n_parts]
    bo_ref, x_ref, mod_ref, gf_ref, rwt_ref, rb_ref, h1_ref, hf_ref, idx_ref, gate_ref = refs[2 * n_parts:]
    row = pl.program_id(0) // tiles_per_batch
    mix = bo_ref[...]
    for a_ref, w_ref in zip(a_refs, w_refs):
        mix = mix + jnp.dot(a_ref[...], w_ref[...], preferred_element_type=F32)
    h1 = x_ref[...] + _mod_row(mod_ref, row, 2) * mix
    h1_ref[...] = h1
    hf = _modulate(h1, gf_ref[...], _mod_row(mod_ref, row, 3), _mod_row(mod_ref, row, 4))
    _store_slabs(hf_ref, hf)
    logits = lax.dot_general(rwt_ref[...], hf.astype(BF16), (((1,), (1,)), ((), ())),
                             preferred_element_type=F32) + rb_ref[...]
    eid = lax.broadcasted_iota(jnp.int32, logits.shape, 0)
    vals, idxs = [], []
    for _ in range(TOP_K):
        m = jnp.max(logits, axis=0, keepdims=True)
        pick = jnp.min(jnp.where(logits == m, eid, N_EXPERTS), axis=0, keepdims=True)
        vals.append(m)
        idxs.append(pick)
        logits = jnp.where(eid == pick, -jnp.inf, logits)
    exps = [jnp.exp(v - vals[0]) for v in vals]
    inv = 1.0 / sum(exps)
    tm = logits.shape[1]
    idx_ref[0] = jnp.concatenate(idxs + [jnp.zeros((8 - TOP_K, tm), jnp.int32)], axis=0)
    gate_ref[0] = jnp.concatenate([e * inv for e in exps] + [jnp.zeros((8 - TOP_K, tm), F32)], axis=0)


def _post(a_parts, w_parts, b_out, x2, mod, gf, rwt_bf, rb_col, seq):
    t, d = x2.shape
    tpb = seq // TM
    nt = t // TM
    const = lambda i: (0, 0)
    tile = lambda i: (i, 0)
    in_specs = ([pl.BlockSpec((TM, a.shape[1]), tile) for a in a_parts]
                + [pl.BlockSpec(w.shape, const) for w in w_parts]
                + [pl.BlockSpec((1, d), const), pl.BlockSpec((TM, d), tile), pl.BlockSpec(mod.shape, const),
                   pl.BlockSpec((1, d), const), pl.BlockSpec(rwt_bf.shape, const),
                   pl.BlockSpec(rb_col.shape, const)])
    return pl.pallas_call(
        functools.partial(_post_kernel, n_parts=len(a_parts), tiles_per_batch=tpb),
        grid=(nt,),
        in_specs=in_specs,
        out_specs=[
            pl.BlockSpec((TM, d), tile),
            pl.BlockSpec((TM * SLABS, 128), tile),
            pl.BlockSpec((1, 8, TM), lambda i: (i, 0, 0)),
            pl.BlockSpec((1, 8, TM), lambda i: (i, 0, 0)),
        ],
        out_shape=[
            jax.ShapeDtypeStruct((t, d), F32),
            jax.ShapeDtypeStruct((t * SLABS, 128), F32),
            jax.ShapeDtypeStruct((nt, 8, TM), jnp.int32),
            jax.ShapeDtypeStruct((nt, 8, TM), F32),
        ],
        compiler_params=_cparams(("arbitrary",)),
        name="post",
    )(*a_parts, *w_parts, b_out, x2, mod, gf, rwt_bf, rb_col)


META_LANES = 128


def _max_blocks(n_tokens):
    return (n_tokens * TOP_K + N_EXPERTS * (EXPERT_BM - 1) + EXPERT_BM - 1) // EXPERT_BM


def _route_kernel(idx_ref, dest_ref, meta_ref, pfull_ref):
    nt, _, tc = idx_ref.shape
    e = N_EXPERTS
    eid = lax.broadcasted_iota(jnp.int32, (e, tc), 0)
    r = lax.broadcasted_iota(jnp.int32, (tc, tc), 0)
    c = lax.broadcasted_iota(jnp.int32, (tc, tc), 1)
    upper = (r < c).astype(BF16)

    def onehots(ci):
        blk = idx_ref[ci]
        return [eid == blk[k:k + 1, :] for k in range(TOP_K)]

    def count_body(ci, carry):
        oh = onehots(ci)
        member = sum(o.astype(F32) for o in oh)
        pfull_ref[ci] = jnp.dot(member.astype(BF16), upper, preferred_element_type=F32) + carry
        return carry + jnp.sum(member, axis=1, keepdims=True)

    counts = lax.fori_loop(0, nt, count_body, jnp.zeros((e, 1), F32))
    nblk = jnp.floor((counts + (EXPERT_BM - 1)) * (1.0 / EXPERT_BM))
    rr = lax.broadcasted_iota(jnp.int32, (e, e), 0)
    cc = lax.broadcasted_iota(jnp.int32, (e, e), 1)
    lower = (cc <= rr).astype(BF16)
    pend_b = jnp.dot(lower, jnp.broadcast_to(nblk, (e, 128)).astype(BF16),
                     preferred_element_type=F32)[:, 0:1]
    pstart = (pend_b - nblk) * EXPERT_BM

    def dest_body(ci, carry):
        oh = onehots(ci)
        base = pfull_ref[ci] + pstart
        rows = [jnp.sum(jnp.where(o, base, 0.0), axis=0, keepdims=True) for o in oh]
        dest_ref[ci] = jnp.concatenate(rows + [jnp.zeros((8 - TOP_K, tc), F32)], axis=0).astype(jnp.int32)
        return carry

    lax.fori_loop(0, nt, dest_body, 0)
    lanes = meta_ref.shape[1]
    bid = lax.broadcasted_iota(jnp.int32, (e, lanes), 1)
    sub = lax.broadcasted_iota(jnp.int32, (e, lanes), 0)
    n_used = jnp.broadcast_to(pend_b[e - 1:e, :], (1, lanes))
    to_lanes = lambda col: jnp.sum(jnp.where(sub == bid, col, 0.0), axis=0, keepdims=True)
    meta_ref[...] = jnp.concatenate([n_used, to_lanes(nblk), to_lanes(pend_b - nblk),
                                     jnp.zeros((5, lanes), F32)], axis=0).astype(jnp.int32)


def _route(idx3):
    nt, _, tc = idx3.shape
    return pl.pallas_call(
        _route_kernel,
        out_shape=[jax.ShapeDtypeStruct((nt, 8, tc), jnp.int32),
                   jax.ShapeDtypeStruct((8, META_LANES), jnp.int32)],
        scratch_shapes=[pltpu.VMEM((nt, N_EXPERTS, tc), F32)],
        compiler_params=pltpu.CompilerParams(vmem_limit_bytes=VMEM_LIMIT),
        name="route",
    )(idx3)


def _invert_kernel(dest_ref, src_ref, zeros_ref, map_ref, sem):
    tm = dest_ref.shape[0] // TOP_K
    step = pl.program_id(0)

    @pl.when(step == 0)
    def _():
        zeros_ref[...] = jnp.zeros(zeros_ref.shape, zeros_ref.dtype)
        cp = pltpu.make_async_copy(zeros_ref, map_ref, sem)
        cp.start()
        cp.wait()

    def body(j, carry):
        for k in range(TOP_K):
            map_ref[dest_ref[k * tm + j]] = (step * tm + j) * TOP_K + k
        return carry

    lax.fori_loop(0, tm, body, 0, unroll=8)

    @pl.when(step == pl.num_programs(0) - 1)
    def _():
        cp = pltpu.make_async_copy(map_ref, src_ref, sem)
        cp.start()
        cp.wait()


def _invert(dest_flat, n_rows):
    return pl.pallas_call(
        _invert_kernel,
        grid=(dest_flat.shape[0] // (TOP_K * TM),),
        in_specs=[pl.BlockSpec((TOP_K * TM,), lambda i: (i,), memory_space=pltpu.SMEM)],
        out_specs=pl.BlockSpec(memory_space=pl.ANY),
        out_shape=jax.ShapeDtypeStruct((n_rows,), jnp.int32),
        scratch_shapes=[pltpu.VMEM((n_rows,), jnp.int32), pltpu.SMEM((n_rows,), jnp.int32),
                        pltpu.SemaphoreType.DMA],
        compiler_params=_cparams(("arbitrary",)),
        name="invert",
    )(dest_flat)


BLOCK_SLAB_ROWS = EXPERT_BM * SLABS
GATHER_AHEAD = 2
GATHER_RING = GATHER_AHEAD + 1


def _expert_kernel(nu_ref, nblk_ref, bstart_ref, src_ref, hf_ref, w1_ref, b1_ref, w2_ref, b2_ref, y_ref,
                   w1b_ref, w2b_ref, xbuf_ref, ybuf_ref, gsem, ysem):
    ex = pl.program_id(0)
    n_used = nu_ref[0]
    n_mine = nblk_ref[ex]
    first = bstart_ref[ex]

    def row_gather(g, j):
        block = jnp.minimum(g, n_used - 1)
        slot = lax.rem(g, GATHER_RING)
        tok = lax.shift_right_logical(src_ref[block * EXPERT_BM + j], 2)
        return pltpu.make_async_copy(hf_ref.at[pl.ds(pl.multiple_of(tok * SLABS, SLABS), SLABS)],
                                     xbuf_ref.at[slot, pl.ds(j * SLABS, SLABS)], gsem.at[slot])

    def start_gather(g):
        for j in range(EXPERT_BM):
            row_gather(g, j).start(priority=1)

    def wait_gather(g):
        for j in range(EXPERT_BM):
            row_gather(g, j).wait()

    def y_copy(block, slot):
        rows = pl.ds(pl.multiple_of(block * BLOCK_SLAB_ROWS, BLOCK_SLAB_ROWS), BLOCK_SLAB_ROWS)
        return pltpu.make_async_copy(ybuf_ref.at[slot], y_ref.at[rows], ysem.at[slot])

    @pl.when(ex == 0)
    def _():
        for g in range(GATHER_AHEAD):
            start_gather(g)

    @pl.when(n_mine > 0)
    def _():
        w1b_ref[...] = w1_ref[0, 0].astype(BF16)
        w2b_ref[...] = w2_ref[0, 0].astype(BF16)

    def block_body(i, carry):
        b = first + i
        slot = lax.rem(b, 2)

        @pl.when(b >= 2)
        def _():
            y_copy(b - 2, slot).wait()

        wait_gather(b)
        x = _load_slabs(xbuf_ref.at[lax.rem(b, GATHER_RING)], EXPERT_BM).astype(BF16)
        start_gather(b + GATHER_AHEAD)
        h = jnp.dot(x, w1b_ref[...], preferred_element_type=F32) + b1_ref[0, 0]
        gate = jnp.minimum(h[:, :EXPERT_FF], SWIGLU_LIMIT)
        up = jnp.clip(h[:, EXPERT_FF:], -SWIGLU_LIMIT, SWIGLU_LIMIT)
        act = (up + 1.0) * gate * _sigmoid(gate * SWIGLU_ALPHA)
        _store_slabs(ybuf_ref.at[slot],
                     jnp.dot(act.astype(BF16), w2b_ref[...], preferred_element_type=F32) + b2_ref[0, 0])
        y_copy(b, slot).start()
        return carry

    lax.fori_loop(0, n_mine, block_body, 0)

    @pl.when(ex == pl.num_programs(0) - 1)
    def _():
        last = n_used - 1
        spare = lax.rem(n_used, 2)
        for g in range(GATHER_AHEAD):
            wait_gather(n_used + g)
        y_copy(last, lax.rem(last, 2)).wait()

        @pl.when(n_used >= 2)
        def _():
            y_copy(last - 1, spare).wait()

        ybuf_ref[0] = jnp.zeros(ybuf_ref.shape[1:], ybuf_ref.dtype)
        n_total = y_ref.shape[0] // BLOCK_SLAB_ROWS

        def tail_start(b, carry):
            y_copy(b, 0).start()
            return carry

        def tail_wait(b, carry):
            y_copy(b, 0).wait()
            return carry

        lax.fori_loop(n_used, n_total, tail_start, 0)
        lax.fori_loop(n_used, n_total, tail_wait, 0)


def _experts(layer, n_used, nblk, bstart, src, hf2, w1, b1, w2, b2):
    n_rows = src.shape[0]
    _, n_exp, d, ff2 = w1.shape
    sel = lambda e, *_: (layer, e, 0, 0)
    return pl.pallas_call(
        _expert_kernel,
        grid_spec=pltpu.PrefetchScalarGridSpec(
            num_scalar_prefetch=4,
            grid=(n_exp,),
            in_specs=[
                pl.BlockSpec(memory_space=pl.ANY),
                pl.BlockSpec((1, 1, d, ff2), sel),
                pl.BlockSpec((1, 1, 1, ff2), sel),
                pl.BlockSpec((1, 1, ff2 // 2, d), sel),
                pl.BlockSpec((1, 1, 1, d), sel),
            ],
            out_specs=pl.BlockSpec(memory_space=pl.ANY),
            scratch_shapes=[
                pltpu.VMEM((d, ff2), BF16),
                pltpu.VMEM((ff2 // 2, d), BF16),
                pltpu.VMEM((GATHER_RING, BLOCK_SLAB_ROWS, 128), F32),
                pltpu.VMEM((2, BLOCK_SLAB_ROWS, 128), F32),
                pltpu.SemaphoreType.DMA((GATHER_RING,)),
                pltpu.SemaphoreType.DMA((2,)),
            ],
        ),
        out_shape=jax.ShapeDtypeStruct((n_rows * SLABS, 128), F32),
        compiler_params=pltpu.CompilerParams(dimension_semantics=("arbitrary",),
                                             vmem_limit_bytes=EXPERT_VMEM_LIMIT),
        name="experts",
    )(n_used, nblk, bstart, src, hf2, w1, b1.reshape(b1.shape[0], n_exp, 1, ff2), w2,
      b2.reshape(b2.shape[0], n_exp, 1, d))


def _combine_kernel(dest_ref, dest_next_ref, gate_ref, h1_ref, mod_ref, y_ref, o_ref, ya_ref, yb_ref, sem,
                    *, tiles_per_batch):
    tm = h1_ref.shape[0]
    step = pl.program_id(0)
    last = pl.num_programs(0) - 1
    row = step // tiles_per_batch

    def row_copy(dref, buf, slot, j, k):
        return pltpu.make_async_copy(y_ref.at[dref[k * tm + j]],
                                     buf.at[k, pl.ds(pl.multiple_of(j * SLABS, SLABS), SLABS)], sem.at[slot])

    def issue(dref, buf, slot):
        def body(j, carry):
            for k in range(TOP_K):
                row_copy(dref, buf, slot, j, k).start(priority=k % 2)
            return carry

        lax.fori_loop(0, tm, body, 0, unroll=True)

    def drain(dref, buf, slot):
        def body(j, carry):
            for k in range(TOP_K):
                row_copy(dref, buf, slot, j, k).wait()
            return carry

        lax.fori_loop(0, tm, body, 0, unroll=True)

    def tile(cur, nxt, slot):
        drain(dest_ref, cur, slot)
        issue(dest_next_ref, nxt, 1 - slot)
        gates = gate_ref[0].T
        acc = gates[:, 0:1] * _load_slabs(cur.at[0], tm)
        for k in range(1, TOP_K):
            acc = acc + gates[:, k:k + 1] * _load_slabs(cur.at[k], tm)
        o_ref[...] = h1_ref[...] + _mod_row(mod_ref, row, 5) * acc

        @pl.when(step == last)
        def _():
            drain(dest_next_ref, nxt, 1 - slot)

    @pl.when(step == 0)
    def _():
        def first(j, carry):
            for k in range(TOP_K):
                pltpu.make_async_copy(y_ref.at[dest_ref[k * tm + j]],
                                      ya_ref.at[k, pl.ds(pl.multiple_of(j * SLABS, SLABS), SLABS)],
                                      sem.at[0]).start(priority=k % 2)
            return carry

        lax.fori_loop(0, tm, first, 0, unroll=8)

    parity = lax.rem(step, 2)
    pl.when(parity == 0)(lambda: tile(ya_ref, yb_ref, 0))
    pl.when(parity == 1)(lambda: tile(yb_ref, ya_ref, 1))


def _combine(dest_flat, gates3, h1, mod, y_rows, seq):
    t, d = h1.shape
    tpb = seq // TM
    nt = t // TM
    buf = pltpu.VMEM((TOP_K, TM * SLABS, 128), F32)
    return pl.pallas_call(
        functools.partial(_combine_kernel, tiles_per_batch=tpb),
        grid=(nt,),
        in_specs=[
            pl.BlockSpec((TOP_K * TM,), lambda i: (i,), memory_space=pltpu.SMEM),
            pl.BlockSpec((TOP_K * TM,), lambda i: (jnp.minimum(i + 1, nt - 1),), memory_space=pltpu.SMEM),
            pl.BlockSpec((1, 8, TM), lambda i: (i, 0, 0)),
            pl.BlockSpec((TM, d), lambda i: (i, 0)),
            pl.BlockSpec(mod.shape, lambda i: (0, 0)),
            pl.BlockSpec(memory_space=pl.ANY),
        ],
        out_specs=pl.BlockSpec((TM, d), lambda i: (i, 0)),
        out_shape=jax.ShapeDtypeStruct((t, d), F32),
        scratch_shapes=[buf, buf, pltpu.SemaphoreType.DMA((2,))],
        compiler_params=_cparams(("arbitrary",)),
        name="combine",
    )(dest_flat, dest_flat, gates3, h1, mod, y_rows)


def _moe(layer, h1, hf2, idx3, gates3, mod, w1, b1, w2, b2, seq):
    dest3, meta = _route(idx3)
    dest_flat = dest3[:, :TOP_K, :].reshape(-1)
    src = _invert(dest_flat, _max_blocks(h1.shape[0]) * EXPERT_BM)
    y2 = _experts(layer, meta[0, :1], meta[1, :N_EXPERTS], meta[2, :N_EXPERTS], src, hf2, w1, b1, w2, b2)
    return _combine(dest_flat, gates3, h1, mod, y2.reshape(-1, SLABS, 128), seq)


def _rope_tables(seq):
    rows = seq // GRID_W
    r = jnp.repeat(jnp.arange(rows), GRID_W).astype(F32)
    col = jnp.tile(jnp.arange(GRID_W), rows).astype(F32)
    pairs = HEAD_DIM // 4
    inv_freq = ROPE_THETA ** (-jnp.arange(pairs, dtype=F32) / pairs)
    ang = jnp.concatenate([r[:, None] * inv_freq, col[:, None] * inv_freq], axis=-1)
    cos, sin = jnp.cos(ang), jnp.sin(ang)
    cos_t = jnp.tile(cos, (1, Q_DIM // (HEAD_DIM // 2)))
    sin_t = jnp.tile(jnp.concatenate([-sin, sin], axis=-1), (1, N_Q_HEADS))
    return cos_t, sin_t


def _block_diag_mean(width):
    i = jnp.arange(width)
    return jnp.where((i[:, None] // HEAD_DIM) == (i[None, :] // HEAD_DIM), 1.0 / HEAD_DIM, 0.0).astype(BF16)


def _head_perm():
    per_kv = N_Q_HEADS // N_KV_HEADS
    heads = []
    for j in range(per_kv):
        heads += [j, per_kv + j]
    return jnp.concatenate([jnp.arange(HEAD_DIM) + h * HEAD_DIM for h in heads])


def kernel(x, c, ctx, c_ctx, ada_w, ada_b, norm_mix_g, norm_ffn_g, ev_w_in, ev_b_in, ev_conv_w, ev_conv_b, ev_conv_ln_g, ev_conv_ln_b, ev_q_norm_g, ev_k_norm_g, ev_sink, ev_w_out, ev_b_out, od_w_in, od_b_in, od_v_ln_g, od_v_ln_b, od_w_s, od_b_s, od_w_out, od_b_out, moe_router_w, moe_router_b, moe_w1, moe_b1, moe_w2, moe_b2):
    bsz, seq, d = x.shape
    ctx_len = ctx.shape[1]
    assert d == D_MODEL and seq % TM == 0 and (bsz * ctx_len) % TM == 0 and bsz < MOD_ROWS
    assert ctx_len % ATT_BLOCK == 0 and seq % GRID_W == 0
    assert ada_w.shape[0] == 2, "layer 0 is the conv/attention layer, layer 1 the gMLP layer"
    ctx_row = bsz
    cc = jnp.zeros((MOD_ROWS, d), F32).at[:bsz].set(c).at[ctx_row].set(c_ctx)
    mods = _ada(cc, ada_w, ada_b)
    row2 = lambda v: v.reshape(1, -1)

    perm = _head_perm()
    w_in = ev_w_in[0]
    w_in_bf = jnp.concatenate([w_in[:, :2 * CONV_CH], w_in[:, 2 * CONV_CH:KV_COL][:, perm], w_in[:, KV_COL:]],
                              axis=1).astype(BF16)
    b_in = ev_b_in[0]
    b_in_p = row2(jnp.concatenate([b_in[:2 * CONV_CH], b_in[2 * CONV_CH:KV_COL][perm], b_in[KV_COL:]]))
    cos_t, sin_t = _rope_tables(seq)
    qg = row2(jnp.tile(ev_q_norm_g[0], N_Q_HEADS))
    kg = row2(jnp.tile(ev_k_norm_g[0], N_KV_HEADS))
    bq, bk = _block_diag_mean(Q_DIM), _block_diag_mean(KV_DIM)
    x2 = x.reshape(bsz * seq, d)
    g_mix0 = row2(norm_mix_g[0])
    u, qa, qb, k, v = _inproj(x2, mods[0], g_mix0, w_in_bf, b_in_p, cos_t, sin_t, qg, kg, bq, bk, seq)
    kx, vx = _ctx_kv(ctx.reshape(bsz * ctx_len, d), mods[0], g_mix0, w_in_bf[:, KV_COL:], b_in_p[:, KV_COL:],
                     kg, bk, ctx_row)
    conv_w = jnp.zeros((32, CONV_CH), F32).at[:CONV_WIDTH].set(ev_conv_w[0])
    conv_h = _conv(u.reshape(bsz, seq, CONV_CH), conv_w, row2(ev_conv_b[0]), row2(ev_conv_ln_g[0]),
                   row2(ev_conv_ln_b[0]))
    r3 = lambda a, w: a.reshape(bsz, -1, w)
    att = _attention(ev_sink[0], r3(qa, Q_DIM), r3(qb, Q_DIM), r3(k, KV_DIM), r3(v, KV_DIM),
                     r3(kx, KV_DIM), r3(vx, KV_DIM))
    w_out = ev_w_out[0]
    w_out_conv = w_out[:CONV_CH].astype(BF16)
    w_out_att = w_out[CONV_CH:][perm].astype(BF16)

    def router(l):
        return moe_router_w[l].T.astype(BF16), moe_router_b[l].reshape(N_EXPERTS, 1)

    rwt, rb = router(0)
    h1, hf, idx3, gates3 = _post([conv_h.reshape(bsz * seq, CONV_CH), att.reshape(bsz * seq, Q_DIM)],
                                 [w_out_conv, w_out_att], row2(ev_b_out[0]), x2, mods[0],
                                 row2(norm_ffn_g[0]), rwt, rb, seq)
    h = _moe(0, h1, hf, idx3, gates3, mods[0], moe_w1, moe_b1, moe_w2, moe_b2, seq)

    a = _gmlp(h, mods[1], row2(norm_mix_g[1]), od_w_in[0].astype(BF16), row2(od_b_in[0]), row2(od_v_ln_g[0]),
              row2(od_v_ln_b[0]), od_w_s[0].astype(BF16), od_b_s[0].T, seq)
    rwt, rb = router(1)
    h1, hf, idx3, gates3 = _post([a], [od_w_out[0].astype(BF16)], row2(od_b_out[0]), h, mods[1],
                                 row2(norm_ffn_g[1]), rwt, rb, seq)
    h = _moe(1, h1, hf, idx3, gates3, mods[1], moe_w1, moe_b1, moe_w2, moe_b2, seq)
    return h.reshape(bsz, seq, d)
```

```python
import functools
import math

import jax
import jax.numpy as jnp
from jax import lax
from jax.experimental import pallas as pl
from jax.experimental.pallas import tpu as pltpu

F32 = jnp.float32
BF16 = jnp.bfloat16

D_MODEL = 1024
HEAD_DIM = 64
N_Q_HEADS = 8
N_KV_HEADS = 2
Q_DIM = N_Q_HEADS * HEAD_DIM
KV_DIM = N_KV_HEADS * HEAD_DIM
CONV_CH = 512
CONV_WIDTH = 31
CONV_HALO = 16
KV_COL = 2 * CONV_CH + Q_DIM
EVEN_IN = KV_COL + 2 * KV_DIM
ATT_BLOCK = 128
GRID_W = 64
ROPE_THETA = 10000.0
CHUNK = 128
GMLP_GROUPS = 8
N_EXPERTS = 32
TOP_K = 4
EXPERT_FF = 1024
SWIGLU_LIMIT = 7.0
SWIGLU_ALPHA = 1.702
EPS = 1e-6
NEG_INF = -1e30

TM = 512
EXPERT_BM = 256
MOD_ROWS = 16
VMEM_LIMIT = 48 * 1024 * 1024
EXPERT_VMEM_LIMIT = 56 * 1024 * 1024


def _cparams(sem):
    return pltpu.CompilerParams(dimension_semantics=sem, vmem_limit_bytes=VMEM_LIMIT)


def _sigmoid(x):
    return 1.0 / (1.0 + jnp.exp(-x))


def _bdot(a, b):
    return jnp.dot(a.astype(BF16), b.astype(BF16), preferred_element_type=F32)


def _ada_kernel(cc_ref, w_ref, b_ref, o_ref):
    a = cc_ref[...]
    o_ref[0] = _bdot(a * _sigmoid(a), w_ref[0]) + b_ref[0]


def _ada(cc, ada_w, ada_b):
    depth, d, n = ada_w.shape
    tn = 1536
    return pl.pallas_call(
        _ada_kernel,
        grid=(depth, n // tn),
        in_specs=[
            pl.BlockSpec((MOD_ROWS, d), lambda l, j: (0, 0)),
            pl.BlockSpec((1, d, tn), lambda l, j: (l, 0, j)),
            pl.BlockSpec((1, 1, tn), lambda l, j: (l, 0, j)),
        ],
        out_specs=pl.BlockSpec((1, MOD_ROWS, tn), lambda l, j: (l, 0, j)),
        out_shape=jax.ShapeDtypeStruct((depth, MOD_ROWS, n), F32),
        compiler_params=_cparams(("arbitrary", "arbitrary")),
        name="ada",
    )(cc, ada_w, ada_b.reshape(depth, 1, n))


def _mod_row(mod_ref, row, part):
    return mod_ref[pl.ds(row, 1), pl.ds(part * D_MODEL, D_MODEL)]


def _modulate(x, g, shift, scale):
    ms = jnp.mean(x * x, axis=-1, keepdims=True)
    return x * lax.rsqrt(ms + EPS) * g * (1.0 + scale) + shift


SLABS = D_MODEL // 128


def _store_slabs(ref, value):
    n = value.shape[0]
    for s in range(SLABS):
        ref[pl.ds(s, n, stride=SLABS), :] = value[:, 128 * s:128 * (s + 1)]


def _load_slabs(ref, n):
    return jnp.concatenate([ref[pl.ds(s, n, stride=SLABS), :] for s in range(SLABS)], axis=1)


def _head_rms(t, blockdiag, g):
    t2 = t * t
    hi = t2.astype(BF16)
    lo = (t2 - hi.astype(F32)).astype(BF16)
    ms = (jnp.dot(hi, blockdiag, preferred_element_type=F32)
          + jnp.dot(lo, blockdiag, preferred_element_type=F32))
    return t * lax.rsqrt(ms + EPS) * g


def _rope(t, cos, sin_signed):
    width = t.shape[-1]
    lane = lax.broadcasted_iota(jnp.int32, t.shape, 1)
    first = (lane % HEAD_DIM) < (HEAD_DIM // 2)
    partner = jnp.where(first, pltpu.roll(t, width - HEAD_DIM // 2, 1), pltpu.roll(t, HEAD_DIM // 2, 1))
    return t * cos + partner * sin_signed


def _inproj_kernel(x_ref, mod_ref, g_ref, w_ref, b_ref, cos_ref, sin_ref, qg_ref, kg_ref, bq_ref, bk_ref,
                   u_ref, qa_ref, qb_ref, k_ref, v_ref, *, tiles_per_batch):
    row = pl.program_id(0) // tiles_per_batch
    hm = _modulate(x_ref[...], g_ref[...], _mod_row(mod_ref, row, 0), _mod_row(mod_ref, row, 1))
    y = _bdot(hm, w_ref[...]) + b_ref[...]
    u_ref[...] = y[:, :CONV_CH] * _sigmoid(y[:, CONV_CH:2 * CONV_CH])
    cos = cos_ref[...]
    sin = sin_ref[...]
    q = _head_rms(y[:, 2 * CONV_CH:KV_COL], bq_ref[...], qg_ref[...])
    q = _rope(q, cos, sin) * (HEAD_DIM ** -0.5)
    lane = lax.broadcasted_iota(jnp.int32, q.shape, 1)
    low = (lane % (2 * HEAD_DIM)) < HEAD_DIM
    qa_ref[...] = jnp.where(low, q, 0.0).astype(BF16)
    qb_ref[...] = jnp.where(low, 0.0, q).astype(BF16)
    k = _head_rms(y[:, KV_COL:KV_COL + KV_DIM], bk_ref[...], kg_ref[...])
    k_ref[...] = _rope(k, cos[:, :KV_DIM], sin[:, :KV_DIM]).astype(BF16)
    v_ref[...] = y[:, KV_COL + KV_DIM:].astype(BF16)


def _inproj(x2, mod, g, w_bf, b, cos_t, sin_t, qg, kg, bq, bk, seq):
    t, d = x2.shape
    tpb = seq // TM
    const = lambda i: (0, 0)
    tile = lambda i: (i, 0)
    pos = lambda i: (i % tpb, 0)
    return pl.pallas_call(
        functools.partial(_inproj_kernel, tiles_per_batch=tpb),
        grid=(t // TM,),
        in_specs=[
            pl.BlockSpec((TM, d), tile),
            pl.BlockSpec(mod.shape, const),
            pl.BlockSpec((1, d), const),
            pl.BlockSpec(w_bf.shape, const),
            pl.BlockSpec((1, EVEN_IN), const),
            pl.BlockSpec((TM, Q_DIM), pos),
            pl.BlockSpec((TM, Q_DIM), pos),
            pl.BlockSpec((1, Q_DIM), const),
            pl.BlockSpec((1, KV_DIM), const),
            pl.BlockSpec((Q_DIM, Q_DIM), const),
            pl.BlockSpec((KV_DIM, KV_DIM), const),
        ],
        out_specs=[
            pl.BlockSpec((TM, CONV_CH), tile),
            pl.BlockSpec((TM, Q_DIM), tile),
            pl.BlockSpec((TM, Q_DIM), tile),
            pl.BlockSpec((TM, KV_DIM), tile),
            pl.BlockSpec((TM, KV_DIM), tile),
        ],
        out_shape=[
            jax.ShapeDtypeStruct((t, CONV_CH), F32),
            jax.ShapeDtypeStruct((t, Q_DIM), BF16),
            jax.ShapeDtypeStruct((t, Q_DIM), BF16),
            jax.ShapeDtypeStruct((t, KV_DIM), BF16),
            jax.ShapeDtypeStruct((t, KV_DIM), BF16),
        ],
        compiler_params=_cparams(("arbitrary",)),
        name="inproj",
    )(x2, mod, g, w_bf, b, cos_t, sin_t, qg, kg, bq, bk)


def _ctx_kv_kernel(g_in_ref, mod_ref, g_ref, w_ref, b_ref, kg_ref, bk_ref, k_ref, v_ref, *, mod_row):
    gm = _modulate(g_in_ref[...], g_ref[...], _mod_row(mod_ref, mod_row, 0), _mod_row(mod_ref, mod_row, 1))
    y = _bdot(gm, w_ref[...]) + b_ref[...]
    k_ref[...] = _head_rms(y[:, :KV_DIM], bk_ref[...], kg_ref[...]).astype(BF16)
    v_ref[...] = y[:, KV_DIM:].astype(BF16)


def _ctx_kv(ctx2, mod, g, w_kv_bf, b_kv, kg, bk, mod_row):
    t, d = ctx2.shape
    const = lambda i: (0, 0)
    tile = lambda i: (i, 0)
    return pl.pallas_call(
        functools.partial(_ctx_kv_kernel, mod_row=mod_row),
        grid=(t // TM,),
        in_specs=[
            pl.BlockSpec((TM, d), tile),
            pl.BlockSpec(mod.shape, const),
            pl.BlockSpec((1, d), const),
            pl.BlockSpec(w_kv_bf.shape, const),
            pl.BlockSpec((1, 2 * KV_DIM), const),
            pl.BlockSpec((1, KV_DIM), const),
            pl.BlockSpec((KV_DIM, KV_DIM), const),
        ],
        out_specs=[pl.BlockSpec((TM, KV_DIM), tile), pl.BlockSpec((TM, KV_DIM), tile)],
        out_shape=[jax.ShapeDtypeStruct((t, KV_DIM), BF16), jax.ShapeDtypeStruct((t, KV_DIM), BF16)],
        compiler_params=_cparams(("arbitrary",)),
        name="ctx_kv",
    )(ctx2, mod, g, w_kv_bf, b_kv, kg, bk)


CONV_ROWS = 64


def _conv_kernel(u_ref, w_ref, cb_ref, lg_ref, lb_ref, o_ref, pad_ref):
    seq = u_ref.shape[1]
    zeros = jnp.zeros((CONV_HALO, CONV_CH), F32)
    pad_ref[pl.ds(0, CONV_HALO), :] = zeros
    pad_ref[pl.ds(CONV_HALO, seq), :] = u_ref[0]
    pad_ref[pl.ds(CONV_HALO + seq, CONV_HALO), :] = zeros
    first = CONV_HALO - CONV_WIDTH // 2

    def body(i, carry):
        t0 = pl.multiple_of(i * CONV_ROWS, CONV_ROWS)
        chunks = []
        for c in range(CONV_CH // 128):
            cols = slice(128 * c, 128 * (c + 1))
            win = pad_ref[pl.ds(t0, CONV_ROWS + 2 * CONV_HALO), cols]
            acc = jnp.zeros((CONV_ROWS, 128), F32) + cb_ref[:, cols]
            for r in range(8):
                rows = CONV_ROWS if r == 0 else CONV_ROWS + 8
                z = None
                for a in range((CONV_WIDTH + first + 7) // 8):
                    tap = 8 * a + r - first
                    if 0 <= tap < CONV_WIDTH:
                        term = w_ref[tap:tap + 1, cols] * win[8 * a:8 * a + rows, :]
                        z = term if z is None else z + term
                acc = acc + z[r:r + CONV_ROWS, :]
            chunks.append(acc)
        acc = jnp.concatenate(chunks, axis=-1)
        mu = jnp.mean(acc, axis=-1, keepdims=True)
        cen = acc - mu
        var = jnp.mean(cen * cen, axis=-1, keepdims=True)
        yn = cen * lax.rsqrt(var + EPS) * lg_ref[...] + lb_ref[...]
        o_ref[0, pl.ds(t0, CONV_ROWS), :] = (yn * _sigmoid(yn)).astype(BF16)
        return carry

    lax.fori_loop(0, seq // CONV_ROWS, body, 0)


def _conv(u3, w_pad, cb, lg, lb):
    bsz, seq, ch = u3.shape
    const = lambda b: (0, 0)
    return pl.pallas_call(
        _conv_kernel,
        grid=(bsz,),
        in_specs=[
            pl.BlockSpec((1, seq, ch), lambda b: (b, 0, 0)),
            pl.BlockSpec(w_pad.shape, const),
            pl.BlockSpec((1, ch), const),
            pl.BlockSpec((1, ch), const),
            pl.BlockSpec((1, ch), const),
        ],
        out_specs=pl.BlockSpec((1, seq, ch), lambda b: (b, 0, 0)),
        out_shape=jax.ShapeDtypeStruct((bsz, seq, ch), BF16),
        scratch_shapes=[pltpu.VMEM((seq + 2 * CONV_HALO, ch), F32)],
        compiler_params=_cparams(("arbitrary",)),
        name="conv",
    )(u3, w_pad, cb, lg, lb)


def _attn_kernel(sink_ref, qa_ref, qb_ref, kp_ref, kc_ref, kn_ref, kx_ref, vp_ref, vc_ref, vn_ref, vx_ref, o_ref):
    n = pl.program_id(1)
    nb = pl.num_programs(1)
    kcat = jnp.concatenate([kp_ref[0], kc_ref[0], kn_ref[0], kx_ref[0]], axis=0)
    vcat = jnp.concatenate([vp_ref[0], vc_ref[0], vn_ref[0], vx_ref[0]], axis=0)
    blk = ATT_BLOCK
    qi = lax.broadcasted_iota(jnp.int32, (blk, blk), 0)
    kj = lax.broadcasted_iota(jnp.int32, (blk, blk), 1)
    mask_prev = (kj >= qi) & (n > 0)
    mask_next = (kj <= qi) & (n < nb - 1)
    n_chunks = Q_DIM // (2 * HEAD_DIM)
    res = []
    for kvh, q_ref in enumerate((qa_ref, qb_ref)):
        qv = q_ref[0]
        q4 = jnp.concatenate([qv[:, 128 * j:128 * (j + 1)] for j in range(n_chunks)], axis=0)
        s = lax.dot_general(q4, kcat, (((1,), (1,)), ((), ())), preferred_element_type=F32)
        outs = []
        for j in range(n_chunks):
            sj = s[blk * j:blk * (j + 1)]
            sp = jnp.where(mask_prev, sj[:, :blk], NEG_INF)
            sc = sj[:, blk:2 * blk]
            sn = jnp.where(mask_next, sj[:, 2 * blk:3 * blk], NEG_INF)
            sx = [sj[:, c0:c0 + blk] for c0 in range(3 * blk, sj.shape[1], blk)]
            sink = sink_ref[kvh * n_chunks + j]
            lanes = [sp, sc, sn] + sx
            m = jnp.maximum(jnp.max(functools.reduce(jnp.maximum, lanes), axis=-1, keepdims=True), sink)
            parts = [jnp.exp(t - m) for t in lanes]
            den = jnp.sum(functools.reduce(jnp.add, parts), axis=-1, keepdims=True) + jnp.exp(sink - m)
            pcat = jnp.concatenate(parts, axis=-1).astype(BF16)
            o = jnp.dot(pcat, vcat, preferred_element_type=F32)
            outs.append(o * (1.0 / den))
        res.append(outs)
    lane = lax.broadcasted_iota(jnp.int32, (blk, 2 * HEAD_DIM), 1)
    low = lane < HEAD_DIM
    o_ref[0] = jnp.concatenate([jnp.where(low, res[0][j], res[1][j]) for j in range(n_chunks)],
                               axis=-1).astype(BF16)


def _attention(sink, qa, qb, k, v, kx, vx):
    bsz, seq, _ = qa.shape
    nb = seq // ATT_BLOCK
    ctx_len = kx.shape[1]
    cur = lambda b, n, s: (b, n, 0)
    prev = lambda b, n, s: (b, jnp.maximum(n - 1, 0), 0)
    nxt = lambda b, n, s: (b, jnp.minimum(n + 1, nb - 1), 0)
    whole = lambda b, n, s: (b, 0, 0)
    kvspec = lambda im: pl.BlockSpec((1, ATT_BLOCK, KV_DIM), im)
    return pl.pallas_call(
        _attn_kernel,
        grid_spec=pltpu.PrefetchScalarGridSpec(
            num_scalar_prefetch=1,
            grid=(bsz, nb),
            in_specs=[
                pl.BlockSpec((1, ATT_BLOCK, Q_DIM), cur),
                pl.BlockSpec((1, ATT_BLOCK, Q_DIM), cur),
                kvspec(prev), kvspec(cur), kvspec(nxt),
                pl.BlockSpec((1, ctx_len, KV_DIM), whole),
                kvspec(prev), kvspec(cur), kvspec(nxt),
                pl.BlockSpec((1, ctx_len, KV_DIM), whole),
            ],
            out_specs=pl.BlockSpec((1, ATT_BLOCK, Q_DIM), cur),
        ),
        out_shape=jax.ShapeDtypeStruct((bsz, seq, Q_DIM), BF16),
        compiler_params=_cparams(("arbitrary", "arbitrary")),
        name="attention",
    )(sink, qa, qb, k, k, k, kx, v, v, v, vx)


def _erf(x):
    return lax.erf(x)


def _gmlp_kernel(x_ref, mod_ref, g_ref, w_ref, b_ref, lg_ref, lb_ref, ws_ref, bst_ref, a_ref, *, tiles_per_batch):
    row = pl.program_id(0) // tiles_per_batch
    hm = _modulate(x_ref[...], g_ref[...], _mod_row(mod_ref, row, 0), _mod_row(mod_ref, row, 1))
    z = _bdot(hm, w_ref[...]) + b_ref[...]
    z = 0.5 * z * (1.0 + _erf(z * (1.0 / math.sqrt(2.0))))
    half = z.shape[1] // 2
    u = z[:, :half]
    v = z[:, half:]
    mu = jnp.mean(v, axis=-1, keepdims=True)
    cen = v - mu
    var = jnp.mean(cen * cen, axis=-1, keepdims=True)
    vn = (cen * lax.rsqrt(var + EPS) * lg_ref[...] + lb_ref[...]).astype(BF16)
    gch = half // GMLP_GROUPS
    for c in range(x_ref.shape[0] // CHUNK):
        rows = slice(c * CHUNK, (c + 1) * CHUNK)
        for g in range(GMLP_GROUPS):
            cols = slice(g * gch, (g + 1) * gch)
            sv = jnp.dot(ws_ref[g], vn[rows, cols], preferred_element_type=F32) + bst_ref[:, g:g + 1]
            a_ref[rows, cols] = (u[rows, cols] * sv).astype(BF16)


def _gmlp(x2, mod, g, w_bf, b, lg, lb, ws_bf, bst, seq):
    t, d = x2.shape
    tpb = seq // TM
    const = lambda i: (0, 0)
    tile = lambda i: (i, 0)
    n_in = w_bf.shape[1]
    return pl.pallas_call(
        functools.partial(_gmlp_kernel, tiles_per_batch=tpb),
        grid=(t // TM,),
        in_specs=[
            pl.BlockSpec((TM, d), tile),
            pl.BlockSpec(mod.shape, const),
            pl.BlockSpec((1, d), const),
            pl.BlockSpec(w_bf.shape, const),
            pl.BlockSpec((1, n_in), const),
            pl.BlockSpec((1, n_in // 2), const),
            pl.BlockSpec((1, n_in // 2), const),
            pl.BlockSpec(ws_bf.shape, lambda i: (0, 0, 0)),
            pl.BlockSpec(bst.shape, const),
        ],
        out_specs=pl.BlockSpec((TM, n_in // 2), tile),
        out_shape=jax.ShapeDtypeStruct((t, n_in // 2), BF16),
        compiler_params=_cparams(("arbitrary",)),
        name="gmlp",
    )(x2, mod, g, w_bf, b, lg, lb, ws_bf, bst)


def _post_kernel(*refs, n_parts, tiles_per_batch):
    a_refs = refs[:n_parts]
    w_refs = refs[n_parts:2 * n_parts]
    bo_ref, x_ref, mod_ref, gf_ref, rwt_ref, rb_ref, h1_ref, hf_ref, idx_ref, gate_ref = refs[2 * n_parts:]
    row = pl.program_id(0) // tiles_per_batch
    mix = bo_ref[...]
    for a_ref, w_ref in zip(a_refs, w_refs):
        mix = mix + jnp.dot(a_ref[...], w_ref[...], preferred_element_type=F32)
    h1 = x_ref[...] + _mod_row(mod_ref, row, 2) * mix
    h1_ref[...] = h1
    hf = _modulate(h1, gf_ref[...], _mod_row(mod_ref, row, 3), _mod_row(mod_ref, row, 4))
    _store_slabs(hf_ref, hf)
    logits = lax.dot_general(rwt_ref[...], hf.astype(BF16), (((1,), (1,)), ((), ())),
                             preferred_element_type=F32) + rb_ref[...]
    eid = lax.broadcasted_iota(jnp.int32, logits.shape, 0)
    vals, idxs = [], []
    for _ in range(TOP_K):
        m = jnp.max(logits, axis=0, keepdims=True)
        pick = jnp.min(jnp.where(logits == m, eid, N_EXPERTS), axis=0, keepdims=True)
        vals.append(m)
        idxs.append(pick)
        logits = jnp.where(eid == pick, -jnp.inf, logits)
    exps = [jnp.exp(v - vals[0]) for v in vals]
    inv = 1.0 / sum(exps)
    tm = logits.shape[1]
    idx_ref[0] = jnp.concatenate(idxs + [jnp.zeros((8 - TOP_K, tm), jnp.int32)], axis=0)
    gate_ref[0] = jnp.concatenate([e * inv for e in exps] + [jnp.zeros((8 - TOP_K, tm), F32)], axis=0)


def _post(a_parts, w_parts, b_out, x2, mod, gf, rwt_bf, rb_col, seq):
    t, d = x2.shape
    tpb = seq // TM
    nt = t // TM
    const = lambda i: (0, 0)
    tile = lambda i: (i, 0)
    in_specs = ([pl.BlockSpec((TM, a.shape[1]), tile) for a in a_parts]
                + [pl.BlockSpec(w.shape, const) for w in w_parts]
                + [pl.BlockSpec((1, d), const), pl.BlockSpec((TM, d), tile), pl.BlockSpec(mod.shape, const),
                   pl.BlockSpec((1, d), const), pl.BlockSpec(rwt_bf.shape, const),
                   pl.BlockSpec(rb_col.shape, const)])
    return pl.pallas_call(
        functools.partial(_post_kernel, n_parts=len(a_parts), tiles_per_batch=tpb),
        grid=(nt,),
        in_specs=in_specs,
        out_specs=[
            pl.BlockSpec((TM, d), tile),
            pl.BlockSpec((TM * SLABS, 128), tile),
            pl.BlockSpec((1, 8, TM), lambda i: (i, 0, 0)),
            pl.BlockSpec((1, 8, TM), lambda i: (i, 0, 0)),
        ],
        out_shape=[
            jax.ShapeDtypeStruct((t, d), F32),
            jax.ShapeDtypeStruct((t * SLABS, 128), F32),
            jax.ShapeDtypeStruct((nt, 8, TM), jnp.int32),
            jax.ShapeDtypeStruct((nt, 8, TM), F32),
        ],
        compiler_params=_cparams(("arbitrary",)),
        name="post",
    )(*a_parts, *w_parts, b_out, x2, mod, gf, rwt_bf, rb_col)


META_LANES = 128


def _max_blocks(n_tokens):
    return (n_tokens * TOP_K + N_EXPERTS * (EXPERT_BM - 1) + EXPERT_BM - 1) // EXPERT_BM


def _route_kernel(idx_ref, dest_ref, meta_ref, pfull_ref):
    nt, _, tc = idx_ref.shape
    e = N_EXPERTS
    eid = lax.broadcasted_iota(jnp.int32, (e, tc), 0)
    r = lax.broadcasted_iota(jnp.int32, (tc, tc), 0)
    c = lax.broadcasted_iota(jnp.int32, (tc, tc), 1)
    upper = (r < c).astype(BF16)

    def onehots(ci):
        blk = idx_ref[ci]
        return [eid == blk[k:k + 1, :] for k in range(TOP_K)]

    def count_body(ci, carry):
        oh = onehots(ci)
        member = sum(o.astype(F32) for o in oh)
        pfull_ref[ci] = jnp.dot(member.astype(BF16), upper, preferred_element_type=F32) + carry
        return carry + jnp.sum(member, axis=1, keepdims=True)

    counts = lax.fori_loop(0, nt, count_body, jnp.zeros((e, 1), F32))
    nblk = jnp.floor((counts + (EXPERT_BM - 1)) * (1.0 / EXPERT_BM))
    rr = lax.broadcasted_iota(jnp.int32, (e, e), 0)
    cc = lax.broadcasted_iota(jnp.int32, (e, e), 1)
    lower = (cc <= rr).astype(BF16)
    pend_b = jnp.dot(lower, jnp.broadcast_to(nblk, (e, 128)).astype(BF16),
                     preferred_element_type=F32)[:, 0:1]
    pstart = (pend_b - nblk) * EXPERT_BM

    def dest_body(ci, carry):
        oh = onehots(ci)
        base = pfull_ref[ci] + pstart
        rows = [jnp.sum(jnp.where(o, base, 0.0), axis=0, keepdims=True) for o in oh]
        dest_ref[ci] = jnp.concatenate(rows + [jnp.zeros((8 - TOP_K, tc), F32)], axis=0).astype(jnp.int32)
        return carry

    lax.fori_loop(0, nt, dest_body, 0)
    lanes = meta_ref.shape[1]
    bid = lax.broadcasted_iota(jnp.int32, (e, lanes), 1)
    sub = lax.broadcasted_iota(jnp.int32, (e, lanes), 0)
    n_used = jnp.broadcast_to(pend_b[e - 1:e, :], (1, lanes))
    to_lanes = lambda col: jnp.sum(jnp.where(sub == bid, col, 0.0), axis=0, keepdims=True)
    meta_ref[...] = jnp.concatenate([n_used, to_lanes(nblk), to_lanes(pend_b - nblk),
                                     jnp.zeros((5, lanes), F32)], axis=0).astype(jnp.int32)


def _route(idx3):
    nt, _, tc = idx3.shape
    return pl.pallas_call(
        _route_kernel,
        out_shape=[jax.ShapeDtypeStruct((nt, 8, tc), jnp.int32),
                   jax.ShapeDtypeStruct((8, META_LANES), jnp.int32)],
        scratch_shapes=[pltpu.VMEM((nt, N_EXPERTS, tc), F32)],
        compiler_params=pltpu.CompilerParams(vmem_limit_bytes=VMEM_LIMIT),
        name="route",
    )(idx3)


def _invert_kernel(dest_ref, src_ref, zeros_ref, map_ref, sem):
    tm = dest_ref.shape[0] // TOP_K
    step = pl.program_id(0)

    @pl.when(step == 0)
    def _():
        zeros_ref[...] = jnp.zeros(zeros_ref.shape, zeros_ref.dtype)
        cp = pltpu.make_async_copy(zeros_ref, map_ref, sem)
        cp.start()
        cp.wait()

    def body(j, carry):
        for k in range(TOP_K):
            map_ref[dest_ref[k * tm + j]] = (step * tm + j) * TOP_K + k
        return carry

    lax.fori_loop(0, tm, body, 0, unroll=8)

    @pl.when(step == pl.num_programs(0) - 1)
    def _():
        cp = pltpu.make_async_copy(map_ref, src_ref, sem)
        cp.start()
        cp.wait()


def _invert(dest_flat, n_rows):
    return pl.pallas_call(
        _invert_kernel,
        grid=(dest_flat.shape[0] // (TOP_K * TM),),
        in_specs=[pl.BlockSpec((TOP_K * TM,), lambda i: (i,), memory_space=pltpu.SMEM)],
        out_specs=pl.BlockSpec(memory_space=pl.ANY),
        out_shape=jax.ShapeDtypeStruct((n_rows,), jnp.int32),
        scratch_shapes=[pltpu.VMEM((n_rows,), jnp.int32), pltpu.SMEM((n_rows,), jnp.int32),
                        pltpu.SemaphoreType.DMA],
        compiler_params=_cparams(("arbitrary",)),
        name="invert",
    )(dest_flat)


BLOCK_SLAB_ROWS = EXPERT_BM * SLABS
GATHER_AHEAD = 6
GATHER_RING = GATHER_AHEAD + 1


def _expert_kernel(nu_ref, nblk_ref, bstart_ref, src_ref, hf_ref, w1_ref, b1_ref, w2_ref, b2_ref, y_ref,
                   w1b_ref, w2b_ref, xbuf_ref, ybuf_ref, gsem, ysem):
    ex = pl.program_id(0)
    n_used = nu_ref[0]
    n_mine = nblk_ref[ex]
    first = bstart_ref[ex]

    def row_gather(g, j):
        block = jnp.minimum(g, n_used - 1)
        slot = lax.rem(g, GATHER_RING)
        tok = lax.shift_right_logical(src_ref[block * EXPERT_BM + j], 2)
        return pltpu.make_async_copy(hf_ref.at[pl.ds(pl.multiple_of(tok * SLABS, SLABS), SLABS)],
                                     xbuf_ref.at[slot, pl.ds(j * SLABS, SLABS)], gsem.at[slot])

    def start_gather(g):
        for j in range(EXPERT_BM):
            row_gather(g, j).start(priority=1)

    def wait_gather(g):
        for j in range(EXPERT_BM):
            row_gather(g, j).wait()

    def y_copy(block, slot):
        rows = pl.ds(pl.multiple_of(block * BLOCK_SLAB_ROWS, BLOCK_SLAB_ROWS), BLOCK_SLAB_ROWS)
        return pltpu.make_async_copy(ybuf_ref.at[slot], y_ref.at[rows], ysem.at[slot])

    @pl.when(ex == 0)
    def _():
        for g in range(GATHER_AHEAD):
            start_gather(g)

    @pl.when(n_mine > 0)
    def _():
        w1b_ref[...] = w1_ref[0, 0].astype(BF16)
        w2b_ref[...] = w2_ref[0, 0].astype(BF16)

    def block_body(i, carry):
        b = first + i
        slot = lax.rem(b, 2)

        @pl.when(b >= 2)
        def _():
            y_copy(b - 2, slot).wait()

        wait_gather(b)
        x = _load_slabs(xbuf_ref.at[lax.rem(b, GATHER_RING)], EXPERT_BM).astype(BF16)
        start_gather(b + GATHER_AHEAD)
        h = jnp.dot(x, w1b_ref[...], preferred_element_type=F32) + b1_ref[0, 0]
        gate = jnp.minimum(h[:, :EXPERT_FF], SWIGLU_LIMIT)
        up = jnp.clip(h[:, EXPERT_FF:], -SWIGLU_LIMIT, SWIGLU_LIMIT)
        act = (up + 1.0) * gate * _sigmoid(gate * SWIGLU_ALPHA)
        _store_slabs(ybuf_ref.at[slot],
                     jnp.dot(act.astype(BF16), w2b_ref[...], preferred_element_type=F32) + b2_ref[0, 0])
        y_copy(b, slot).start()
        return carry

    lax.fori_loop(0, n_mine, block_body, 0)

    @pl.when(ex == pl.num_programs(0) - 1)
    def _():
        last = n_used - 1
        spare = lax.rem(n_used, 2)
        for g in range(GATHER_AHEAD):
            wait_gather(n_used + g)
        y_copy(last, lax.rem(last, 2)).wait()

        @pl.when(n_used >= 2)
        def _():
            y_copy(last - 1, spare).wait()

        ybuf_ref[0] = jnp.zeros(ybuf_ref.shape[1:], ybuf_ref.dtype)
        n_total = y_ref.shape[0] // BLOCK_SLAB_ROWS

        def tail_start(b, carry):
            y_copy(b, 0).start()
            return carry

        def tail_wait(b, carry):
            y_copy(b, 0).wait()
            return carry

        lax.fori_loop(n_used, n_total, tail_start, 0)
        lax.fori_loop(n_used, n_total, tail_wait, 0)


def _experts(layer, n_used, nblk, bstart, src, hf2, w1, b1, w2, b2):
    n_rows = src.shape[0]
    _, n_exp, d, ff2 = w1.shape
    sel = lambda e, *_: (layer, e, 0, 0)
    return pl.pallas_call(
        _expert_kernel,
        grid_spec=pltpu.PrefetchScalarGridSpec(
            num_scalar_prefetch=4,
            grid=(n_exp,),
            in_specs=[
                pl.BlockSpec(memory_space=pl.ANY),
                pl.BlockSpec((1, 1, d, ff2), sel),
                pl.BlockSpec((1, 1, 1, ff2), sel),
                pl.BlockSpec((1, 1, ff2 // 2, d), sel),
                pl.BlockSpec((1, 1, 1, d), sel),
            ],
            out_specs=pl.BlockSpec(memory_space=pl.ANY),
            scratch_shapes=[
                pltpu.VMEM((d, ff2), BF16),
                pltpu.VMEM((ff2 // 2, d), BF16),
                pltpu.VMEM((GATHER_RING, BLOCK_SLAB_ROWS, 128), F32),
                pltpu.VMEM((2, BLOCK_SLAB_ROWS, 128), F32),
                pltpu.SemaphoreType.DMA((GATHER_RING,)),
                pltpu.SemaphoreType.DMA((2,)),
            ],
        ),
        out_shape=jax.ShapeDtypeStruct((n_rows * SLABS, 128), F32),
        compiler_params=pltpu.CompilerParams(dimension_semantics=("arbitrary",),
                                             vmem_limit_bytes=EXPERT_VMEM_LIMIT),
        name="experts",
    )(n_used, nblk, bstart, src, hf2, w1, b1.reshape(b1.shape[0], n_exp, 1, ff2), w2,
      b2.reshape(b2.shape[0], n_exp, 1, d))


def _combine_kernel(dest_ref, dest_next_ref, gate_ref, h1_ref, mod_ref, y_ref, o_ref, ya_ref, yb_ref, sem,
                    *, tiles_per_batch):
    tm = h1_ref.shape[0]
    step = pl.program_id(0)
    last = pl.num_programs(0) - 1
    row = step // tiles_per_batch

    def row_copy(dref, buf, slot, j, k):
        return pltpu.make_async_copy(y_ref.at[dref[k * tm + j]],
                                     buf.at[k, pl.ds(pl.multiple_of(j * SLABS, SLABS), SLABS)], sem.at[slot])

    def issue(dref, buf, slot):
        def body(j, carry):
            for k in range(TOP_K):
                row_copy(dref, buf, slot, j, k).start(priority=k % 2)
            return carry

        lax.fori_loop(0, tm, body, 0, unroll=True)

    def drain(dref, buf, slot):
        def body(j, carry):
            for k in range(TOP_K):
                row_copy(dref, buf, slot, j, k).wait()
            return carry

        lax.fori_loop(0, tm, body, 0, unroll=True)

    def tile(cur, nxt, slot):
        drain(dest_ref, cur, slot)
        issue(dest_next_ref, nxt, 1 - slot)
        gates = gate_ref[0].T
        acc = gates[:, 0:1] * _load_slabs(cur.at[0], tm)
        for k in range(1, TOP_K):
            acc = acc + gates[:, k:k + 1] * _load_slabs(cur.at[k], tm)
        o_ref[...] = h1_ref[...] + _mod_row(mod_ref, row, 5) * acc

        @pl.when(step == last)
        def _():
            drain(dest_next_ref, nxt, 1 - slot)

    @pl.when(step == 0)
    def _():
        def first(j, carry):
            for k in range(TOP_K):
                pltpu.make_async_copy(y_ref.at[dest_ref[k * tm + j]],
                                      ya_ref.at[k, pl.ds(pl.multiple_of(j * SLABS, SLABS), SLABS)],
                                      sem.at[0]).start(priority=k % 2)
            return carry

        lax.fori_loop(0, tm, first, 0, unroll=8)

    parity = lax.rem(step, 2)
    pl.when(parity == 0)(lambda: tile(ya_ref, yb_ref, 0))
    pl.when(parity == 1)(lambda: tile(yb_ref, ya_ref, 1))


def _combine(dest_flat, gates3, h1, mod, y_rows, seq):
    t, d = h1.shape
    tpb = seq // TM
    nt = t // TM
    buf = pltpu.VMEM((TOP_K, TM * SLABS, 128), F32)
    return pl.pallas_call(
        functools.partial(_combine_kernel, tiles_per_batch=tpb),
        grid=(nt,),
        in_specs=[
            pl.BlockSpec((TOP_K * TM,), lambda i: (i,), memory_space=pltpu.SMEM),
            pl.BlockSpec((TOP_K * TM,), lambda i: (jnp.minimum(i + 1, nt - 1),), memory_space=pltpu.SMEM),
            pl.BlockSpec((1, 8, TM), lambda i: (i, 0, 0)),
            pl.BlockSpec((TM, d), lambda i: (i, 0)),
            pl.BlockSpec(mod.shape, lambda i: (0, 0)),
            pl.BlockSpec(memory_space=pl.ANY),
        ],
        out_specs=pl.BlockSpec((TM, d), lambda i: (i, 0)),
        out_shape=jax.ShapeDtypeStruct((t, d), F32),
        scratch_shapes=[buf, buf, pltpu.SemaphoreType.DMA((2,))],
        compiler_params=_cparams(("arbitrary",)),
        name="combine",
    )(dest_flat, dest_flat, gates3, h1, mod, y_rows)


def _moe(layer, h1, hf2, idx3, gates3, mod, w1, b1, w2, b2, seq):
    dest3, meta = _route(idx3)
    dest_flat = dest3[:, :TOP_K, :].reshape(-1)
    src = _invert(dest_flat, _max_blocks(h1.shape[0]) * EXPERT_BM)
    y2 = _experts(layer, meta[0, :1], meta[1, :N_EXPERTS], meta[2, :N_EXPERTS], src, hf2, w1, b1, w2, b2)
    return _combine(dest_flat, gates3, h1, mod, y2.reshape(-1, SLABS, 128), seq)


def _rope_tables(seq):
    rows = seq // GRID_W
    r = jnp.repeat(jnp.arange(rows), GRID_W).astype(F32)
    col = jnp.tile(jnp.arange(GRID_W), rows).astype(F32)
    pairs = HEAD_DIM // 4
    inv_freq = ROPE_THETA ** (-jnp.arange(pairs, dtype=F32) / pairs)
    ang = jnp.concatenate([r[:, None] * inv_freq, col[:, None] * inv_freq], axis=-1)
    cos, sin = jnp.cos(ang), jnp.sin(ang)
    cos_t = jnp.tile(cos, (1, Q_DIM // (HEAD_DIM // 2)))
    sin_t = jnp.tile(jnp.concatenate([-sin, sin], axis=-1), (1, N_Q_HEADS))
    return cos_t, sin_t


def _block_diag_mean(width):
    i = jnp.arange(width)
    return jnp.where((i[:, None] // HEAD_DIM) == (i[None, :] // HEAD_DIM), 1.0 / HEAD_DIM, 0.0).astype(BF16)


def _head_perm():
    per_kv = N_Q_HEADS // N_KV_HEADS
    heads = []
    for j in range(per_kv):
        heads += [j, per_kv + j]
    return jnp.concatenate([jnp.arange(HEAD_DIM) + h * HEAD_DIM for h in heads])


def kernel(x, c, ctx, c_ctx, ada_w, ada_b, norm_mix_g, norm_ffn_g, ev_w_in, ev_b_in, ev_conv_w, ev_conv_b, ev_conv_ln_g, ev_conv_ln_b, ev_q_norm_g, ev_k_norm_g, ev_sink, ev_w_out, ev_b_out, od_w_in, od_b_in, od_v_ln_g, od_v_ln_b, od_w_s, od_b_s, od_w_out, od_b_out, moe_router_w, moe_router_b, moe_w1, moe_b1, moe_w2, moe_b2):
    bsz, seq, d = x.shape
    ctx_len = ctx.shape[1]
    assert d == D_MODEL and seq % TM == 0 and (bsz * ctx_len) % TM == 0 and bsz < MOD_ROWS
    assert ctx_len % ATT_BLOCK == 0 and seq % GRID_W == 0
    assert ada_w.shape[0] == 2, "layer 0 is the conv/attention layer, layer 1 the gMLP layer"
    ctx_row = bsz
    cc = jnp.zeros((MOD_ROWS, d), F32).at[:bsz].set(c).at[ctx_row].set(c_ctx)
    mods = _ada(cc, ada_w, ada_b)
    row2 = lambda v: v.reshape(1, -1)

    perm = _head_perm()
    w_in = ev_w_in[0]
    w_in_bf = jnp.concatenate([w_in[:, :2 * CONV_CH], w_in[:, 2 * CONV_CH:KV_COL][:, perm], w_in[:, KV_COL:]],
                              axis=1).astype(BF16)
    b_in = ev_b_in[0]
    b_in_p = row2(jnp.concatenate([b_in[:2 * CONV_CH], b_in[2 * CONV_CH:KV_COL][perm], b_in[KV_COL:]]))
    cos_t, sin_t = _rope_tables(seq)
    qg = row2(jnp.tile(ev_q_norm_g[0], N_Q_HEADS))
    kg = row2(jnp.tile(ev_k_norm_g[0], N_KV_HEADS))
    bq, bk = _block_diag_mean(Q_DIM), _block_diag_mean(KV_DIM)
    x2 = x.reshape(bsz * seq, d)
    g_mix0 = row2(norm_mix_g[0])
    u, qa, qb, k, v = _inproj(x2, mods[0], g_mix0, w_in_bf, b_in_p, cos_t, sin_t, qg, kg, bq, bk, seq)
    kx, vx = _ctx_kv(ctx.reshape(bsz * ctx_len, d), mods[0], g_mix0, w_in_bf[:, KV_COL:], b_in_p[:, KV_COL:],
                     kg, bk, ctx_row)
    conv_w = jnp.zeros((32, CONV_CH), F32).at[:CONV_WIDTH].set(ev_conv_w[0])
    conv_h = _conv(u.reshape(bsz, seq, CONV_CH), conv_w, row2(ev_conv_b[0]), row2(ev_conv_ln_g[0]),
                   row2(ev_conv_ln_b[0]))
    r3 = lambda a, w: a.reshape(bsz, -1, w)
    att = _attention(ev_sink[0], r3(qa, Q_DIM), r3(qb, Q_DIM), r3(k, KV_DIM), r3(v, KV_DIM),
                     r3(kx, KV_DIM), r3(vx, KV_DIM))
    w_out = ev_w_out[0]
    w_out_conv = w_out[:CONV_CH].astype(BF16)
    w_out_att = w_out[CONV_CH:][perm].astype(BF16)

    def router(l):
        return moe_router_w[l].T.astype(BF16), moe_router_b[l].reshape(N_EXPERTS, 1)

    rwt, rb = router(0)
    h1, hf, idx3, gates3 = _post([conv_h.reshape(bsz * seq, CONV_CH), att.reshape(bsz * seq, Q_DIM)],
                                 [w_out_conv, w_out_att], row2(ev_b_out[0]), x2, mods[0],
                                 row2(norm_ffn_g[0]), rwt, rb, seq)
    h = _moe(0, h1, hf, idx3, gates3, mods[0], moe_w1, moe_b1, moe_w2, moe_b2, seq)

    a = _gmlp(h, mods[1], row2(norm_mix_g[1]), od_w_in[0].astype(BF16), row2(od_b_in[0]), row2(od_v_ln_g[0]),
              row2(od_v_ln_b[0]), od_w_s[0].astype(BF16), od_b_s[0].T, seq)
    rwt, rb = router(1)
    h1, hf, idx3, gates3 = _post([a], [od_w_out[0].astype(BF16)], row2(od_b_out[0]), h, mods[1],
                                 row2(norm_ffn_g[1]), rwt, rb, seq)
    h = _moe(1, h1, hf, idx3, gates3, mods[1], moe_w1, moe_b1, moe_w2, moe_b2, seq)
    return h.reshape(bsz, seq, d)
```

```python
import functools
import math

import jax
import jax.numpy as jnp
from jax import lax
from jax.experimental import pallas as pl
from jax.experimental.pallas import tpu as pltpu

F32 = jnp.float32
BF16 = jnp.bfloat16

D_MODEL = 1024
HEAD_DIM = 64
N_Q_HEADS = 8
N_KV_HEADS = 2
Q_DIM = N_Q_HEADS * HEAD_DIM
KV_DIM = N_KV_HEADS * HEAD_DIM
CONV_CH = 512
CONV_WIDTH = 31
CONV_HALO = 16
KV_COL = 2 * CONV_CH + Q_DIM
EVEN_IN = KV_COL + 2 * KV_DIM
ATT_BLOCK = 128
GRID_W = 64
ROPE_THETA = 10000.0
CHUNK = 128
GMLP_GROUPS = 8
N_EXPERTS = 32
TOP_K = 4
EXPERT_FF = 1024
SWIGLU_LIMIT = 7.0
SWIGLU_ALPHA = 1.702
EPS = 1e-6
NEG_INF = -1e30

TM = 512
EXPERT_BM = 256
MOD_ROWS = 16
VMEM_LIMIT = 48 * 1024 * 1024
EXPERT_VMEM_LIMIT = 56 * 1024 * 1024


def _cparams(sem):
    return pltpu.CompilerParams(dimension_semantics=sem, vmem_limit_bytes=VMEM_LIMIT)


def _sigmoid(x):
    return 1.0 / (1.0 + jnp.exp(-x))


def _bdot(a, b):
    return jnp.dot(a.astype(BF16), b.astype(BF16), preferred_element_type=F32)


def _ada_kernel(cc_ref, w_ref, b_ref, o_ref):
    a = cc_ref[...]
    o_ref[0] = _bdot(a * _sigmoid(a), w_ref[0]) + b_ref[0]


def _ada(cc, ada_w, ada_b):
    depth, d, n = ada_w.shape
    tn = 1536
    return pl.pallas_call(
        _ada_kernel,
        grid=(depth, n // tn),
        in_specs=[
            pl.BlockSpec((MOD_ROWS, d), lambda l, j: (0, 0)),
            pl.BlockSpec((1, d, tn), lambda l, j: (l, 0, j)),
            pl.BlockSpec((1, 1, tn), lambda l, j: (l, 0, j)),
        ],
        out_specs=pl.BlockSpec((1, MOD_ROWS, tn), lambda l, j: (l, 0, j)),
        out_shape=jax.ShapeDtypeStruct((depth, MOD_ROWS, n), F32),
        compiler_params=_cparams(("arbitrary", "arbitrary")),
        name="ada",
    )(cc, ada_w, ada_b.reshape(depth, 1, n))


def _mod_row(mod_ref, row, part):
    return mod_ref[pl.ds(row, 1), pl.ds(part * D_MODEL, D_MODEL)]


def _modulate(x, g, shift, scale):
    ms = jnp.mean(x * x, axis=-1, keepdims=True)
    return x * lax.rsqrt(ms + EPS) * g * (1.0 + scale) + shift


SLABS = D_MODEL // 128


def _store_slabs(ref, value):
    n = value.shape[0]
    for s in range(SLABS):
        ref[pl.ds(s, n, stride=SLABS), :] = value[:, 128 * s:128 * (s + 1)]


def _load_slabs(ref, n):
    return jnp.concatenate([ref[pl.ds(s, n, stride=SLABS), :] for s in range(SLABS)], axis=1)


def _head_rms(t, blockdiag, g):
    t2 = t * t
    hi = t2.astype(BF16)
    lo = (t2 - hi.astype(F32)).astype(BF16)
    ms = (jnp.dot(hi, blockdiag, preferred_element_type=F32)
          + jnp.dot(lo, blockdiag, preferred_element_type=F32))
    return t * lax.rsqrt(ms + EPS) * g


def _rope(t, cos, sin_signed):
    width = t.shape[-1]
    lane = lax.broadcasted_iota(jnp.int32, t.shape, 1)
    first = (lane % HEAD_DIM) < (HEAD_DIM // 2)
    partner = jnp.where(first, pltpu.roll(t, width - HEAD_DIM // 2, 1), pltpu.roll(t, HEAD_DIM // 2, 1))
    return t * cos + partner * sin_signed


def _inproj_kernel(x_ref, mod_ref, g_ref, w_ref, b_ref, cos_ref, sin_ref, qg_ref, kg_ref, bq_ref, bk_ref,
                   u_ref, qa_ref, qb_ref, k_ref, v_ref, *, tiles_per_batch):
    row = pl.program_id(0) // tiles_per_batch
    hm = _modulate(x_ref[...], g_ref[...], _mod_row(mod_ref, row, 0), _mod_row(mod_ref, row, 1))
    y = _bdot(hm, w_ref[...]) + b_ref[...]
    u_ref[...] = y[:, :CONV_CH] * _sigmoid(y[:, CONV_CH:2 * CONV_CH])
    cos = cos_ref[...]
    sin = sin_ref[...]
    q = _head_rms(y[:, 2 * CONV_CH:KV_COL], bq_ref[...], qg_ref[...])
    q = _rope(q, cos, sin) * (HEAD_DIM ** -0.5)
    lane = lax.broadcasted_iota(jnp.int32, q.shape, 1)
    low = (lane % (2 * HEAD_DIM)) < HEAD_DIM
    qa_ref[...] = jnp.where(low, q, 0.0).astype(BF16)
    qb_ref[...] = jnp.where(low, 0.0, q).astype(BF16)
    k = _head_rms(y[:, KV_COL:KV_COL + KV_DIM], bk_ref[...], kg_ref[...])
    k_ref[...] = _rope(k, cos[:, :KV_DIM], sin[:, :KV_DIM]).astype(BF16)
    v_ref[...] = y[:, KV_COL + KV_DIM:].astype(BF16)


def _inproj(x2, mod, g, w_bf, b, cos_t, sin_t, qg, kg, bq, bk, seq):
    t, d = x2.shape
    tpb = seq // TM
    const = lambda i: (0, 0)
    tile = lambda i: (i, 0)
    pos = lambda i: (i % tpb, 0)
    return pl.pallas_call(
        functools.partial(_inproj_kernel, tiles_per_batch=tpb),
        grid=(t // TM,),
        in_specs=[
            pl.BlockSpec((TM, d), tile),
            pl.BlockSpec(mod.shape, const),
            pl.BlockSpec((1, d), const),
            pl.BlockSpec(w_bf.shape, const),
            pl.BlockSpec((1, EVEN_IN), const),
            pl.BlockSpec((TM, Q_DIM), pos),
            pl.BlockSpec((TM, Q_DIM), pos),
            pl.BlockSpec((1, Q_DIM), const),
            pl.BlockSpec((1, KV_DIM), const),
            pl.BlockSpec((Q_DIM, Q_DIM), const),
            pl.BlockSpec((KV_DIM, KV_DIM), const),
        ],
        out_specs=[
            pl.BlockSpec((TM, CONV_CH), tile),
            pl.BlockSpec((TM, Q_DIM), tile),
            pl.BlockSpec((TM, Q_DIM), tile),
            pl.BlockSpec((TM, KV_DIM), tile),
            pl.BlockSpec((TM, KV_DIM), tile),
        ],
        out_shape=[
            jax.ShapeDtypeStruct((t, CONV_CH), F32),
            jax.ShapeDtypeStruct((t, Q_DIM), BF16),
            jax.ShapeDtypeStruct((t, Q_DIM), BF16),
            jax.ShapeDtypeStruct((t, KV_DIM), BF16),
            jax.ShapeDtypeStruct((t, KV_DIM), BF16),
        ],
        compiler_params=_cparams(("arbitrary",)),
        name="inproj",
    )(x2, mod, g, w_bf, b, cos_t, sin_t, qg, kg, bq, bk)


def _ctx_kv_kernel(g_in_ref, mod_ref, g_ref, w_ref, b_ref, kg_ref, bk_ref, k_ref, v_ref, *, mod_row):
    gm = _modulate(g_in_ref[...], g_ref[...], _mod_row(mod_ref, mod_row, 0), _mod_row(mod_ref, mod_row, 1))
    y = _bdot(gm, w_ref[...]) + b_ref[...]
    k_ref[...] = _head_rms(y[:, :KV_DIM], bk_ref[...], kg_ref[...]).astype(BF16)
    v_ref[...] = y[:, KV_DIM:].astype(BF16)


def _ctx_kv(ctx2, mod, g, w_kv_bf, b_kv, kg, bk, mod_row):
    t, d = ctx2.shape
    const = lambda i: (0, 0)
    tile = lambda i: (i, 0)
    return pl.pallas_call(
        functools.partial(_ctx_kv_kernel, mod_row=mod_row),
        grid=(t // TM,),
        in_specs=[
            pl.BlockSpec((TM, d), tile),
            pl.BlockSpec(mod.shape, const),
            pl.BlockSpec((1, d), const),
            pl.BlockSpec(w_kv_bf.shape, const),
            pl.BlockSpec((1, 2 * KV_DIM), const),
            pl.BlockSpec((1, KV_DIM), const),
            pl.BlockSpec((KV_DIM, KV_DIM), const),
        ],
        out_specs=[pl.BlockSpec((TM, KV_DIM), tile), pl.BlockSpec((TM, KV_DIM), tile)],
        out_shape=[jax.ShapeDtypeStruct((t, KV_DIM), BF16), jax.ShapeDtypeStruct((t, KV_DIM), BF16)],
        compiler_params=_cparams(("arbitrary",)),
        name="ctx_kv",
    )(ctx2, mod, g, w_kv_bf, b_kv, kg, bk)


CONV_ROWS = 64


def _conv_kernel(u_ref, w_ref, cb_ref, lg_ref, lb_ref, o_ref, pad_ref):
    seq = u_ref.shape[1]
    zeros = jnp.zeros((CONV_HALO, CONV_CH), F32)
    pad_ref[pl.ds(0, CONV_HALO), :] = zeros
    pad_ref[pl.ds(CONV_HALO, seq), :] = u_ref[0]
    pad_ref[pl.ds(CONV_HALO + seq, CONV_HALO), :] = zeros
    first = CONV_HALO - CONV_WIDTH // 2

    def body(i, carry):
        t0 = pl.multiple_of(i * CONV_ROWS, CONV_ROWS)
        chunks = []
        for c in range(CONV_CH // 128):
            cols = slice(128 * c, 128 * (c + 1))
            win = pad_ref[pl.ds(t0, CONV_ROWS + 2 * CONV_HALO), cols]
            acc = jnp.zeros((CONV_ROWS, 128), F32) + cb_ref[:, cols]
            for r in range(8):
                rows = CONV_ROWS if r == 0 else CONV_ROWS + 8
                z = None
                for a in range((CONV_WIDTH + first + 7) // 8):
                    tap = 8 * a + r - first
                    if 0 <= tap < CONV_WIDTH:
                        term = w_ref[tap:tap + 1, cols] * win[8 * a:8 * a + rows, :]
                        z = term if z is None else z + term
                acc = acc + z[r:r + CONV_ROWS, :]
            chunks.append(acc)
        acc = jnp.concatenate(chunks, axis=-1)
        mu = jnp.mean(acc, axis=-1, keepdims=True)
        cen = acc - mu
        var = jnp.mean(cen * cen, axis=-1, keepdims=True)
        yn = cen * lax.rsqrt(var + EPS) * lg_ref[...] + lb_ref[...]
        o_ref[0, pl.ds(t0, CONV_ROWS), :] = (yn * _sigmoid(yn)).astype(BF16)
        return carry

    lax.fori_loop(0, seq // CONV_ROWS, body, 0)


def _conv(u3, w_pad, cb, lg, lb):
    bsz, seq, ch = u3.shape
    const = lambda b: (0, 0)
    return pl.pallas_call(
        _conv_kernel,
        grid=(bsz,),
        in_specs=[
            pl.BlockSpec((1, seq, ch), lambda b: (b, 0, 0)),
            pl.BlockSpec(w_pad.shape, const),
            pl.BlockSpec((1, ch), const),
            pl.BlockSpec((1, ch), const),
            pl.BlockSpec((1, ch), const),
        ],
        out_specs=pl.BlockSpec((1, seq, ch), lambda b: (b, 0, 0)),
        out_shape=jax.ShapeDtypeStruct((bsz, seq, ch), BF16),
        scratch_shapes=[pltpu.VMEM((seq + 2 * CONV_HALO, ch), F32)],
        compiler_params=_cparams(("arbitrary",)),
        name="conv",
    )(u3, w_pad, cb, lg, lb)


def _attn_kernel(sink_ref, qa_ref, qb_ref, kp_ref, kc_ref, kn_ref, kx_ref, vp_ref, vc_ref, vn_ref, vx_ref, o_ref):
    n = pl.program_id(1)
    nb = pl.num_programs(1)
    kcat = jnp.concatenate([kp_ref[0], kc_ref[0], kn_ref[0], kx_ref[0]], axis=0)
    vcat = jnp.concatenate([vp_ref[0], vc_ref[0], vn_ref[0], vx_ref[0]], axis=0)
    blk = ATT_BLOCK
    qi = lax.broadcasted_iota(jnp.int32, (blk, blk), 0)
    kj = lax.broadcasted_iota(jnp.int32, (blk, blk), 1)
    mask_prev = (kj >= qi) & (n > 0)
    mask_next = (kj <= qi) & (n < nb - 1)
    n_chunks = Q_DIM // (2 * HEAD_DIM)
    res = []
    for kvh, q_ref in enumerate((qa_ref, qb_ref)):
        qv = q_ref[0]
        q4 = jnp.concatenate([qv[:, 128 * j:128 * (j + 1)] for j in range(n_chunks)], axis=0)
        s = lax.dot_general(q4, kcat, (((1,), (1,)), ((), ())), preferred_element_type=F32)
        outs = []
        for j in range(n_chunks):
            sj = s[blk * j:blk * (j + 1)]
            sp = jnp.where(mask_prev, sj[:, :blk], NEG_INF)
            sc = sj[:, blk:2 * blk]
            sn = jnp.where(mask_next, sj[:, 2 * blk:3 * blk], NEG_INF)
            sx = [sj[:, c0:c0 + blk] for c0 in range(3 * blk, sj.shape[1], blk)]
            sink = sink_ref[kvh * n_chunks + j]
            lanes = [sp, sc, sn] + sx
            m = jnp.maximum(jnp.max(functools.reduce(jnp.maximum, lanes), axis=-1, keepdims=True), sink)
            parts = [jnp.exp(t - m) for t in lanes]
            den = jnp.sum(functools.reduce(jnp.add, parts), axis=-1, keepdims=True) + jnp.exp(sink - m)
            pcat = jnp.concatenate(parts, axis=-1).astype(BF16)
            o = jnp.dot(pcat, vcat, preferred_element_type=F32)
            outs.append(o * (1.0 / den))
        res.append(outs)
    lane = lax.broadcasted_iota(jnp.int32, (blk, 2 * HEAD_DIM), 1)
    low = lane < HEAD_DIM
    o_ref[0] = jnp.concatenate([jnp.where(low, res[0][j], res[1][j]) for j in range(n_chunks)],
                               axis=-1).astype(BF16)


def _attention(sink, qa, qb, k, v, kx, vx):
    bsz, seq, _ = qa.shape
    nb = seq // ATT_BLOCK
    ctx_len = kx.shape[1]
    cur = lambda b, n, s: (b, n, 0)
    prev = lambda b, n, s: (b, jnp.maximum(n - 1, 0), 0)
    nxt = lambda b, n, s: (b, jnp.minimum(n + 1, nb - 1), 0)
    whole = lambda b, n, s: (b, 0, 0)
    kvspec = lambda im: pl.BlockSpec((1, ATT_BLOCK, KV_DIM), im)
    return pl.pallas_call(
        _attn_kernel,
        grid_spec=pltpu.PrefetchScalarGridSpec(
            num_scalar_prefetch=1,
            grid=(bsz, nb),
            in_specs=[
                pl.BlockSpec((1, ATT_BLOCK, Q_DIM), cur),
                pl.BlockSpec((1, ATT_BLOCK, Q_DIM), cur),
                kvspec(prev), kvspec(cur), kvspec(nxt),
                pl.BlockSpec((1, ctx_len, KV_DIM), whole),
                kvspec(prev), kvspec(cur), kvspec(nxt),
                pl.BlockSpec((1, ctx_len, KV_DIM), whole),
            ],
            out_specs=pl.BlockSpec((1, ATT_BLOCK, Q_DIM), cur),
        ),
        out_shape=jax.ShapeDtypeStruct((bsz, seq, Q_DIM), BF16),
        compiler_params=_cparams(("arbitrary", "arbitrary")),
        name="attention",
    )(sink, qa, qb, k, k, k, kx, v, v, v, vx)


def _erf(x):
    return lax.erf(x)


def _gmlp_tile(x, row, mod_ref, g_ref, w_ref, b_ref, lg_ref, lb_ref, ws_ref, bst_ref, a_ref):
    hm = _modulate(x, g_ref[...], _mod_row(mod_ref, row, 0), _mod_row(mod_ref, row, 1))
    z = _bdot(hm, w_ref[...]) + b_ref[...]
    z = 0.5 * z * (1.0 + _erf(z * (1.0 / math.sqrt(2.0))))
    half = z.shape[1] // 2
    u = z[:, :half]
    v = z[:, half:]
    mu = jnp.mean(v, axis=-1, keepdims=True)
    cen = v - mu
    var = jnp.mean(cen * cen, axis=-1, keepdims=True)
    vn = (cen * lax.rsqrt(var + EPS) * lg_ref[...] + lb_ref[...]).astype(BF16)
    gch = half // GMLP_GROUPS
    for c in range(x.shape[0] // CHUNK):
        rows = slice(c * CHUNK, (c + 1) * CHUNK)
        for g in range(GMLP_GROUPS):
            cols = slice(g * gch, (g + 1) * gch)
            sv = jnp.dot(ws_ref[g], vn[rows, cols], preferred_element_type=F32) + bst_ref[:, g:g + 1]
            a_ref[rows, cols] = (u[rows, cols] * sv).astype(BF16)


def _post_kernel(*refs, n_parts, tiles_per_batch):
    a_refs = refs[:n_parts]
    w_refs = refs[n_parts:2 * n_parts]
    bo_ref, x_ref, mod_ref, gf_ref, rwt_ref, rb_ref, h1_ref, hf_ref, idx_ref, gate_ref = refs[2 * n_parts:]
    row = pl.program_id(0) // tiles_per_batch
    mix = bo_ref[...]
    for a_ref, w_ref in zip(a_refs, w_refs):
        mix = mix + jnp.dot(a_ref[...], w_ref[...], preferred_element_type=F32)
    h1 = x_ref[...] + _mod_row(mod_ref, row, 2) * mix
    h1_ref[...] = h1
    hf = _modulate(h1, gf_ref[...], _mod_row(mod_ref, row, 3), _mod_row(mod_ref, row, 4))
    _store_slabs(hf_ref, hf)
    logits = lax.dot_general(rwt_ref[...], hf.astype(BF16), (((1,), (1,)), ((), ())),
                             preferred_element_type=F32) + rb_ref[...]
    eid = lax.broadcasted_iota(jnp.int32, logits.shape, 0)
    vals, idxs = [], []
    for _ in range(TOP_K):
        m = jnp.max(logits, axis=0, keepdims=True)
        pick = jnp.min(jnp.where(logits == m, eid, N_EXPERTS), axis=0, keepdims=True)
        vals.append(m)
        idxs.append(pick)
        logits = jnp.where(eid == pick, -jnp.inf, logits)
    exps = [jnp.exp(v - vals[0]) for v in vals]
    inv = 1.0 / sum(exps)
    tm = logits.shape[1]
    idx_ref[0] = jnp.concatenate(idxs + [jnp.zeros((8 - TOP_K, tm), jnp.int32)], axis=0)
    gate_ref[0] = jnp.concatenate([e * inv for e in exps] + [jnp.zeros((8 - TOP_K, tm), F32)], axis=0)


def _post(a_parts, w_parts, b_out, x2, mod, gf, rwt_bf, rb_col, seq):
    t, d = x2.shape
    tpb = seq // TM
    nt = t // TM
    const = lambda i: (0, 0)
    tile = lambda i: (i, 0)
    in_specs = ([pl.BlockSpec((TM, a.shape[1]), tile) for a in a_parts]
                + [pl.BlockSpec(w.shape, const) for w in w_parts]
                + [pl.BlockSpec((1, d), const), pl.BlockSpec((TM, d), tile), pl.BlockSpec(mod.shape, const),
                   pl.BlockSpec((1, d), const), pl.BlockSpec(rwt_bf.shape, const),
                   pl.BlockSpec(rb_col.shape, const)])
    return pl.pallas_call(
        functools.partial(_post_kernel, n_parts=len(a_parts), tiles_per_batch=tpb),
        grid=(nt,),
        in_specs=in_specs,
        out_specs=[
            pl.BlockSpec((TM, d), tile),
            pl.BlockSpec((TM * SLABS, 128), tile),
            pl.BlockSpec((1, 8, TM), lambda i: (i, 0, 0)),
            pl.BlockSpec((1, 8, TM), lambda i: (i, 0, 0)),
        ],
        out_shape=[
            jax.ShapeDtypeStruct((t, d), F32),
            jax.ShapeDtypeStruct((t * SLABS, 128), F32),
            jax.ShapeDtypeStruct((nt, 8, TM), jnp.int32),
            jax.ShapeDtypeStruct((nt, 8, TM), F32),
        ],
        compiler_params=_cparams(("arbitrary",)),
        name="post",
    )(*a_parts, *w_parts, b_out, x2, mod, gf, rwt_bf, rb_col)


META_LANES = 128


def _max_blocks(n_tokens):
    return (n_tokens * TOP_K + N_EXPERTS * (EXPERT_BM - 1) + EXPERT_BM - 1) // EXPERT_BM


def _route_kernel(idx_ref, dest_ref, meta_ref, pfull_ref):
    nt, _, tc = idx_ref.shape
    e = N_EXPERTS
    eid = lax.broadcasted_iota(jnp.int32, (e, tc), 0)
    r = lax.broadcasted_iota(jnp.int32, (tc, tc), 0)
    c = lax.broadcasted_iota(jnp.int32, (tc, tc), 1)
    upper = (r < c).astype(BF16)

    def onehots(ci):
        blk = idx_ref[ci]
        return [eid == blk[k:k + 1, :] for k in range(TOP_K)]

    def count_body(ci, carry):
        oh = onehots(ci)
        member = sum(o.astype(F32) for o in oh)
        pfull_ref[ci] = jnp.dot(member.astype(BF16), upper, preferred_element_type=F32) + carry
        return carry + jnp.sum(member, axis=1, keepdims=True)

    counts = lax.fori_loop(0, nt, count_body, jnp.zeros((e, 1), F32))
    nblk = jnp.floor((counts + (EXPERT_BM - 1)) * (1.0 / EXPERT_BM))
    rr = lax.broadcasted_iota(jnp.int32, (e, e), 0)
    cc = lax.broadcasted_iota(jnp.int32, (e, e), 1)
    lower = (cc <= rr).astype(BF16)
    pend_b = jnp.dot(lower, jnp.broadcast_to(nblk, (e, 128)).astype(BF16),
                     preferred_element_type=F32)[:, 0:1]
    pstart = (pend_b - nblk) * EXPERT_BM

    def dest_body(ci, carry):
        oh = onehots(ci)
        base = pfull_ref[ci] + pstart
        rows = [jnp.sum(jnp.where(o, base, 0.0), axis=0, keepdims=True) for o in oh]
        dest_ref[ci] = jnp.concatenate(rows + [jnp.zeros((8 - TOP_K, tc), F32)], axis=0).astype(jnp.int32)
        return carry

    lax.fori_loop(0, nt, dest_body, 0)
    lanes = meta_ref.shape[1]
    bid = lax.broadcasted_iota(jnp.int32, (e, lanes), 1)
    sub = lax.broadcasted_iota(jnp.int32, (e, lanes), 0)
    n_used = jnp.broadcast_to(pend_b[e - 1:e, :], (1, lanes))
    to_lanes = lambda col: jnp.sum(jnp.where(sub == bid, col, 0.0), axis=0, keepdims=True)
    meta_ref[...] = jnp.concatenate([n_used, to_lanes(nblk), to_lanes(pend_b - nblk),
                                     jnp.zeros((5, lanes), F32)], axis=0).astype(jnp.int32)


def _route(idx3):
    nt, _, tc = idx3.shape
    return pl.pallas_call(
        _route_kernel,
        out_shape=[jax.ShapeDtypeStruct((nt, 8, tc), jnp.int32),
                   jax.ShapeDtypeStruct((8, META_LANES), jnp.int32)],
        scratch_shapes=[pltpu.VMEM((nt, N_EXPERTS, tc), F32)],
        compiler_params=pltpu.CompilerParams(vmem_limit_bytes=VMEM_LIMIT),
        name="route",
    )(idx3)


def _invert_kernel(dest_ref, src_ref, zeros_ref, map_ref, sem):
    tm = dest_ref.shape[0] // TOP_K
    step = pl.program_id(0)

    @pl.when(step == 0)
    def _():
        zeros_ref[...] = jnp.zeros(zeros_ref.shape, zeros_ref.dtype)
        cp = pltpu.make_async_copy(zeros_ref, map_ref, sem)
        cp.start()
        cp.wait()

    def body(j, carry):
        for k in range(TOP_K):
            map_ref[dest_ref[k * tm + j]] = (step * tm + j) * TOP_K + k
        return carry

    lax.fori_loop(0, tm, body, 0, unroll=8)

    @pl.when(step == pl.num_programs(0) - 1)
    def _():
        cp = pltpu.make_async_copy(map_ref, src_ref, sem)
        cp.start()
        cp.wait()


def _invert(dest_flat, n_rows):
    return pl.pallas_call(
        _invert_kernel,
        grid=(dest_flat.shape[0] // (TOP_K * TM),),
        in_specs=[pl.BlockSpec((TOP_K * TM,), lambda i: (i,), memory_space=pltpu.SMEM)],
        out_specs=pl.BlockSpec(memory_space=pl.ANY),
        out_shape=jax.ShapeDtypeStruct((n_rows,), jnp.int32),
        scratch_shapes=[pltpu.VMEM((n_rows,), jnp.int32), pltpu.SMEM((n_rows,), jnp.int32),
                        pltpu.SemaphoreType.DMA],
        compiler_params=_cparams(("arbitrary",)),
        name="invert",
    )(dest_flat)


BLOCK_SLAB_ROWS = EXPERT_BM * SLABS
GATHER_AHEAD = 6
GATHER_RING = GATHER_AHEAD + 1


def _expert_kernel(nu_ref, nblk_ref, bstart_ref, src_ref, hf_ref, w1_ref, b1_ref, w2_ref, b2_ref, y_ref,
                   w1b_ref, w2b_ref, xbuf_ref, ybuf_ref, gsem, ysem):
    ex = pl.program_id(0)
    n_used = nu_ref[0]
    n_mine = nblk_ref[ex]
    first = bstart_ref[ex]

    def row_gather(g, j):
        block = jnp.minimum(g, n_used - 1)
        slot = lax.rem(g, GATHER_RING)
        tok = lax.shift_right_logical(src_ref[block * EXPERT_BM + j], 2)
        return pltpu.make_async_copy(hf_ref.at[pl.ds(pl.multiple_of(tok * SLABS, SLABS), SLABS)],
                                     xbuf_ref.at[slot, pl.ds(j * SLABS, SLABS)], gsem.at[slot])

    def start_gather(g):
        for j in range(EXPERT_BM):
            row_gather(g, j).start(priority=1)

    def wait_gather(g):
        for j in range(EXPERT_BM):
            row_gather(g, j).wait()

    def y_copy(block, slot):
        rows = pl.ds(pl.multiple_of(block * BLOCK_SLAB_ROWS, BLOCK_SLAB_ROWS), BLOCK_SLAB_ROWS)
        return pltpu.make_async_copy(ybuf_ref.at[slot], y_ref.at[rows], ysem.at[slot])

    @pl.when(ex == 0)
    def _():
        for g in range(GATHER_AHEAD):
            start_gather(g)

    @pl.when(n_mine > 0)
    def _():
        w1b_ref[...] = w1_ref[0, 0].astype(BF16)
        w2b_ref[...] = w2_ref[0, 0].astype(BF16)

    def block_body(i, carry):
        b = first + i
        slot = lax.rem(b, 2)

        @pl.when(b >= 2)
        def _():
            y_copy(b - 2, slot).wait()

        wait_gather(b)
        x = _load_slabs(xbuf_ref.at[lax.rem(b, GATHER_RING)], EXPERT_BM).astype(BF16)
        start_gather(b + GATHER_AHEAD)
        h = jnp.dot(x, w1b_ref[...], preferred_element_type=F32) + b1_ref[0, 0]
        gate = jnp.minimum(h[:, :EXPERT_FF], SWIGLU_LIMIT)
        up = jnp.clip(h[:, EXPERT_FF:], -SWIGLU_LIMIT, SWIGLU_LIMIT)
        act = (up + 1.0) * gate * _sigmoid(gate * SWIGLU_ALPHA)
        _store_slabs(ybuf_ref.at[slot],
                     jnp.dot(act.astype(BF16), w2b_ref[...], preferred_element_type=F32) + b2_ref[0, 0])
        y_copy(b, slot).start()
        return carry

    lax.fori_loop(0, n_mine, block_body, 0)

    @pl.when(ex == pl.num_programs(0) - 1)
    def _():
        last = n_used - 1
        spare = lax.rem(n_used, 2)
        for g in range(GATHER_AHEAD):
            wait_gather(n_used + g)
        y_copy(last, lax.rem(last, 2)).wait()

        @pl.when(n_used >= 2)
        def _():
            y_copy(last - 1, spare).wait()

        ybuf_ref[0] = jnp.zeros(ybuf_ref.shape[1:], ybuf_ref.dtype)
        n_total = y_ref.shape[0] // BLOCK_SLAB_ROWS

        def tail_start(b, carry):
            y_copy(b, 0).start()
            return carry

        def tail_wait(b, carry):
            y_copy(b, 0).wait()
            return carry

        lax.fori_loop(n_used, n_total, tail_start, 0)
        lax.fori_loop(n_used, n_total, tail_wait, 0)


def _experts(layer, n_used, nblk, bstart, src, hf2, w1, b1, w2, b2):
    n_rows = src.shape[0]
    _, n_exp, d, ff2 = w1.shape
    sel = lambda e, *_: (layer, e, 0, 0)
    return pl.pallas_call(
        _expert_kernel,
        grid_spec=pltpu.PrefetchScalarGridSpec(
            num_scalar_prefetch=4,
            grid=(n_exp,),
            in_specs=[
                pl.BlockSpec(memory_space=pl.ANY),
                pl.BlockSpec((1, 1, d, ff2), sel),
                pl.BlockSpec((1, 1, 1, ff2), sel),
                pl.BlockSpec((1, 1, ff2 // 2, d), sel),
                pl.BlockSpec((1, 1, 1, d), sel),
            ],
            out_specs=pl.BlockSpec(memory_space=pl.ANY),
            scratch_shapes=[
                pltpu.VMEM((d, ff2), BF16),
                pltpu.VMEM((ff2 // 2, d), BF16),
                pltpu.VMEM((GATHER_RING, BLOCK_SLAB_ROWS, 128), F32),
                pltpu.VMEM((2, BLOCK_SLAB_ROWS, 128), F32),
                pltpu.SemaphoreType.DMA((GATHER_RING,)),
                pltpu.SemaphoreType.DMA((2,)),
            ],
        ),
        out_shape=jax.ShapeDtypeStruct((n_rows * SLABS, 128), F32),
        compiler_params=pltpu.CompilerParams(dimension_semantics=("arbitrary",),
                                             vmem_limit_bytes=EXPERT_VMEM_LIMIT),
        name="experts",
    )(n_used, nblk, bstart, src, hf2, w1, b1.reshape(b1.shape[0], n_exp, 1, ff2), w2,
      b2.reshape(b2.shape[0], n_exp, 1, d))


def _combine_tiles(dest_ref, dest_next_ref, gate_ref, h1_ref, mod_ref, y_ref, o_ref, ya_ref, yb_ref, sem,
                   tiles_per_batch, then=None):
    tm = h1_ref.shape[0]
    step = pl.program_id(0)
    last = pl.num_programs(0) - 1
    row = step // tiles_per_batch

    def row_copy(dref, buf, slot, j, k):
        return pltpu.make_async_copy(y_ref.at[dref[k * tm + j]],
                                     buf.at[k, pl.ds(pl.multiple_of(j * SLABS, SLABS), SLABS)], sem.at[slot])

    def issue(dref, buf, slot):
        def body(j, carry):
            for k in range(TOP_K):
                row_copy(dref, buf, slot, j, k).start(priority=k % 2)
            return carry

        lax.fori_loop(0, tm, body, 0, unroll=True)

    def drain(dref, buf, slot):
        def body(j, carry):
            for k in range(TOP_K):
                row_copy(dref, buf, slot, j, k).wait()
            return carry

        lax.fori_loop(0, tm, body, 0, unroll=True)

    def tile(cur, nxt, slot):
        drain(dest_ref, cur, slot)
        issue(dest_next_ref, nxt, 1 - slot)
        gates = gate_ref[0].T
        acc = gates[:, 0:1] * _load_slabs(cur.at[0], tm)
        for k in range(1, TOP_K):
            acc = acc + gates[:, k:k + 1] * _load_slabs(cur.at[k], tm)
        h = h1_ref[...] + _mod_row(mod_ref, row, 5) * acc
        o_ref[...] = h
        if then is not None:
            then(h, row)

        @pl.when(step == last)
        def _():
            drain(dest_next_ref, nxt, 1 - slot)

    @pl.when(step == 0)
    def _():
        def first(j, carry):
            for k in range(TOP_K):
                pltpu.make_async_copy(y_ref.at[dest_ref[k * tm + j]],
                                      ya_ref.at[k, pl.ds(pl.multiple_of(j * SLABS, SLABS), SLABS)],
                                      sem.at[0]).start(priority=k % 2)
            return carry

        lax.fori_loop(0, tm, first, 0, unroll=8)

    parity = lax.rem(step, 2)
    pl.when(parity == 0)(lambda: tile(ya_ref, yb_ref, 0))
    pl.when(parity == 1)(lambda: tile(yb_ref, ya_ref, 1))


def _combine_kernel(dest_ref, dest_next_ref, gate_ref, h1_ref, mod_ref, y_ref, o_ref, ya_ref, yb_ref, sem,
                    *, tiles_per_batch):
    _combine_tiles(dest_ref, dest_next_ref, gate_ref, h1_ref, mod_ref, y_ref, o_ref, ya_ref, yb_ref, sem,
                   tiles_per_batch)


def _combine_gmlp_kernel(dest_ref, dest_next_ref, gate_ref, h1_ref, mod_ref, y_ref,
                         mod1_ref, g_ref, w_ref, b_ref, lg_ref, lb_ref, ws_ref, bst_ref,
                         o_ref, a_ref, ya_ref, yb_ref, sem, *, tiles_per_batch):
    gmlp = functools.partial(_gmlp_tile, mod_ref=mod1_ref, g_ref=g_ref, w_ref=w_ref, b_ref=b_ref, lg_ref=lg_ref,
                             lb_ref=lb_ref, ws_ref=ws_ref, bst_ref=bst_ref, a_ref=a_ref)
    _combine_tiles(dest_ref, dest_next_ref, gate_ref, h1_ref, mod_ref, y_ref, o_ref, ya_ref, yb_ref, sem,
                   tiles_per_batch, then=gmlp)


def _combine(dest_flat, gates3, h1, mod, y_rows, seq, gmlp_args=None):
    t, d = h1.shape
    tpb = seq // TM
    nt = t // TM
    const = lambda i: (0, 0)
    tile = lambda i: (i, 0)
    buf = pltpu.VMEM((TOP_K, TM * SLABS, 128), F32)
    in_specs = [
        pl.BlockSpec((TOP_K * TM,), lambda i: (i,), memory_space=pltpu.SMEM),
        pl.BlockSpec((TOP_K * TM,), lambda i: (jnp.minimum(i + 1, nt - 1),), memory_space=pltpu.SMEM),
        pl.BlockSpec((1, 8, TM), lambda i: (i, 0, 0)),
        pl.BlockSpec((TM, d), tile),
        pl.BlockSpec(mod.shape, const),
        pl.BlockSpec(memory_space=pl.ANY),
    ]
    out_specs = [pl.BlockSpec((TM, d), tile)]
    out_shape = [jax.ShapeDtypeStruct((t, d), F32)]
    args = [dest_flat, dest_flat, gates3, h1, mod, y_rows]
    if gmlp_args is None:
        body, name = _combine_kernel, "combine"
    else:
        body, name = _combine_gmlp_kernel, "combine_gmlp"
        mod1, g, w_bf, b, lg, lb, ws_bf, bst = gmlp_args
        n_half = w_bf.shape[1] // 2
        in_specs += [pl.BlockSpec(mod1.shape, const), pl.BlockSpec((1, d), const), pl.BlockSpec(w_bf.shape, const),
                     pl.BlockSpec((1, 2 * n_half), const), pl.BlockSpec((1, n_half), const),
                     pl.BlockSpec((1, n_half), const), pl.BlockSpec(ws_bf.shape, lambda i: (0, 0, 0)),
                     pl.BlockSpec(bst.shape, const)]
        out_specs.append(pl.BlockSpec((TM, n_half), tile))
        out_shape.append(jax.ShapeDtypeStruct((t, n_half), BF16))
        args += list(gmlp_args)
    return pl.pallas_call(
        functools.partial(body, tiles_per_batch=tpb),
        grid=(nt,),
        in_specs=in_specs,
        out_specs=out_specs,
        out_shape=out_shape,
        scratch_shapes=[buf, buf, pltpu.SemaphoreType.DMA((2,))],
        compiler_params=pltpu.CompilerParams(dimension_semantics=("arbitrary",),
                                             vmem_limit_bytes=EXPERT_VMEM_LIMIT),
        name=name,
    )(*args)


def _moe(layer, h1, hf2, idx3, gates3, mod, w1, b1, w2, b2, seq, gmlp_args=None):
    dest3, meta = _route(idx3)
    dest_flat = dest3[:, :TOP_K, :].reshape(-1)
    src = _invert(dest_flat, _max_blocks(h1.shape[0]) * EXPERT_BM)
    y2 = _experts(layer, meta[0, :1], meta[1, :N_EXPERTS], meta[2, :N_EXPERTS], src, hf2, w1, b1, w2, b2)
    return _combine(dest_flat, gates3, h1, mod, y2.reshape(-1, SLABS, 128), seq, gmlp_args)


def _rope_tables(seq):
    rows = seq // GRID_W
    r = jnp.repeat(jnp.arange(rows), GRID_W).astype(F32)
    col = jnp.tile(jnp.arange(GRID_W), rows).astype(F32)
    pairs = HEAD_DIM // 4
    inv_freq = ROPE_THETA ** (-jnp.arange(pairs, dtype=F32) / pairs)
    ang = jnp.concatenate([r[:, None] * inv_freq, col[:, None] * inv_freq], axis=-1)
    cos, sin = jnp.cos(ang), jnp.sin(ang)
    cos_t = jnp.tile(cos, (1, Q_DIM // (HEAD_DIM // 2)))
    sin_t = jnp.tile(jnp.concatenate([-sin, sin], axis=-1), (1, N_Q_HEADS))
    return cos_t, sin_t


def _block_diag_mean(width):
    i = jnp.arange(width)
    return jnp.where((i[:, None] // HEAD_DIM) == (i[None, :] // HEAD_DIM), 1.0 / HEAD_DIM, 0.0).astype(BF16)


def _head_perm():
    per_kv = N_Q_HEADS // N_KV_HEADS
    heads = []
    for j in range(per_kv):
        heads += [j, per_kv + j]
    return jnp.concatenate([jnp.arange(HEAD_DIM) + h * HEAD_DIM for h in heads])


def kernel(x, c, ctx, c_ctx, ada_w, ada_b, norm_mix_g, norm_ffn_g, ev_w_in, ev_b_in, ev_conv_w, ev_conv_b, ev_conv_ln_g, ev_conv_ln_b, ev_q_norm_g, ev_k_norm_g, ev_sink, ev_w_out, ev_b_out, od_w_in, od_b_in, od_v_ln_g, od_v_ln_b, od_w_s, od_b_s, od_w_out, od_b_out, moe_router_w, moe_router_b, moe_w1, moe_b1, moe_w2, moe_b2):
    bsz, seq, d = x.shape
    ctx_len = ctx.shape[1]
    assert d == D_MODEL and seq % TM == 0 and (bsz * ctx_len) % TM == 0 and bsz < MOD_ROWS
    assert ctx_len % ATT_BLOCK == 0 and seq % GRID_W == 0
    assert ada_w.shape[0] == 2, "layer 0 is the conv/attention layer, layer 1 the gMLP layer"
    ctx_row = bsz
    cc = jnp.zeros((MOD_ROWS, d), F32).at[:bsz].set(c).at[ctx_row].set(c_ctx)
    mods = _ada(cc, ada_w, ada_b)
    row2 = lambda v: v.reshape(1, -1)

    perm = _head_perm()
    w_in = ev_w_in[0]
    w_in_bf = jnp.concatenate([w_in[:, :2 * CONV_CH], w_in[:, 2 * CONV_CH:KV_COL][:, perm], w_in[:, KV_COL:]],
                              axis=1).astype(BF16)
    b_in = ev_b_in[0]
    b_in_p = row2(jnp.concatenate([b_in[:2 * CONV_CH], b_in[2 * CONV_CH:KV_COL][perm], b_in[KV_COL:]]))
    cos_t, sin_t = _rope_tables(seq)
    qg = row2(jnp.tile(ev_q_norm_g[0], N_Q_HEADS))
    kg = row2(jnp.tile(ev_k_norm_g[0], N_KV_HEADS))
    bq, bk = _block_diag_mean(Q_DIM), _block_diag_mean(KV_DIM)
    x2 = x.reshape(bsz * seq, d)
    g_mix0 = row2(norm_mix_g[0])
    u, qa, qb, k, v = _inproj(x2, mods[0], g_mix0, w_in_bf, b_in_p, cos_t, sin_t, qg, kg, bq, bk, seq)
    kx, vx = _ctx_kv(ctx.reshape(bsz * ctx_len, d), mods[0], g_mix0, w_in_bf[:, KV_COL:], b_in_p[:, KV_COL:],
                     kg, bk, ctx_row)
    conv_w = jnp.zeros((32, CONV_CH), F32).at[:CONV_WIDTH].set(ev_conv_w[0])
    conv_h = _conv(u.reshape(bsz, seq, CONV_CH), conv_w, row2(ev_conv_b[0]), row2(ev_conv_ln_g[0]),
                   row2(ev_conv_ln_b[0]))
    r3 = lambda a, w: a.reshape(bsz, -1, w)
    att = _attention(ev_sink[0], r3(qa, Q_DIM), r3(qb, Q_DIM), r3(k, KV_DIM), r3(v, KV_DIM),
                     r3(kx, KV_DIM), r3(vx, KV_DIM))
    w_out = ev_w_out[0]
    w_out_conv = w_out[:CONV_CH].astype(BF16)
    w_out_att = w_out[CONV_CH:][perm].astype(BF16)

    def router(l):
        return moe_router_w[l].T.astype(BF16), moe_router_b[l].reshape(N_EXPERTS, 1)

    rwt, rb = router(0)
    h1, hf, idx3, gates3 = _post([conv_h.reshape(bsz * seq, CONV_CH), att.reshape(bsz * seq, Q_DIM)],
                                 [w_out_conv, w_out_att], row2(ev_b_out[0]), x2, mods[0],
                                 row2(norm_ffn_g[0]), rwt, rb, seq)
    gmlp_args = (mods[1], row2(norm_mix_g[1]), od_w_in[0].astype(BF16), row2(od_b_in[0]), row2(od_v_ln_g[0]),
                 row2(od_v_ln_b[0]), od_w_s[0].astype(BF16), od_b_s[0].T)
    h, a = _moe(0, h1, hf, idx3, gates3, mods[0], moe_w1, moe_b1, moe_w2, moe_b2, seq, gmlp_args)
    rwt, rb = router(1)
    h1, hf, idx3, gates3 = _post([a], [od_w_out[0].astype(BF16)], row2(od_b_out[0]), h, mods[1],
                                 row2(norm_ffn_g[1]), rwt, rb, seq)
    (h,) = _moe(1, h1, hf, idx3, gates3, mods[1], moe_w1, moe_b1, moe_w2, moe_b2, seq)
    return h.reshape(bsz, seq, d)
```
